```python
import math
import jax, jax.numpy as jnp
from jax import lax
import numpy as np


D_MODEL = 1024
BATCH = 2
SEQ = 8192
DEPTH = 2

N_A = DEPTH // 2
N_B = DEPTH - N_A
RET_HEADS = 4
RET_DK = D_MODEL // RET_HEADS
RET_DV = 2 * RET_DK
RET_CHUNK = 128
RET_ROT_BASE = 10000.0
NSA_HEADS = 16
NSA_GROUPS = 4
NSA_HD = D_MODEL // NSA_HEADS
NSA_REP = NSA_HEADS // NSA_GROUPS
L_CMP = 32
CMP_STRIDE = 16
L_SLC = 64
N_SELECT = 16
WINDOW = 512
CMP_HIDDEN = 4 * NSA_HD
Q_BLOCK = 128
ROPE_THETA = 500000.0
ROPE_DIMS = NSA_HD // 4
D_FF = -(-(8 * D_MODEL) // (3 * 256)) * 256
EPS = 1e-6
NEG = -1e30
BIG = 1e9

kernel_name = 'yoco_retnet_nsa_hybrid'


def rmsnorm(x, g):
    xf = x.astype(jnp.float32)
    y = xf * lax.rsqrt(jnp.mean(xf * xf, axis=-1, keepdims=True) + EPS)
    return (y * g.astype(jnp.float32)).astype(x.dtype)


def apply_rotary(x, freqs):
    n = freqs.shape[0]
    pos = jnp.arange(x.shape[1], dtype=jnp.float32)
    ang = pos[:, None] * freqs[None, :]
    cos = jnp.cos(ang)[None, :, None, :]
    sin = jnp.sin(ang)[None, :, None, :]
    xf = x.astype(jnp.float32)
    x1 = xf[..., :n]
    x2 = xf[..., n:2 * n]
    out = jnp.concatenate([x1 * cos - x2 * sin, x1 * sin + x2 * cos, xf[..., 2 * n:]], axis=-1)
    return out.astype(x.dtype)


def partial_rope_freqs():
    return ROPE_THETA ** (-jnp.arange(0, ROPE_DIMS, 2, dtype=jnp.float32) / ROPE_DIMS)


def swiglu(h, w_in, w_out):
    g, u = jnp.split(h @ w_in, 2, axis=-1)
    return (jax.nn.silu(g) * u) @ w_out


def retention(h, w_in, gn, w_out):
    B, T, _ = h.shape
    Hh, dk, dv, C = RET_HEADS, RET_DK, RET_DV, RET_CHUNK
    n = T // C
    q, k, v, g = jnp.split(h @ w_in, [Hh * dk, 2 * Hh * dk, 2 * Hh * dk + Hh * dv], axis=-1)
    freqs = 1.0 / (RET_ROT_BASE ** jnp.linspace(0.0, 1.0, dk // 2, dtype=jnp.float32))
    q = apply_rotary(q.reshape(B, T, Hh, dk), freqs).astype(jnp.float32)
    k = apply_rotary(k.reshape(B, T, Hh, dk), freqs).astype(jnp.float32) * (dk ** -0.5)
    v = v.reshape(B, T, Hh, dv).astype(jnp.float32)

    def chunks(a):
        return a.reshape(B, n, C, Hh, a.shape[-1]).transpose(1, 0, 3, 2, 4)

    qc, kc, vc = chunks(q), chunks(k), chunks(v)
    log_g = jnp.log1p(-jnp.exp2(-5.0 - jnp.arange(Hh, dtype=jnp.float32)))
    idx = jnp.arange(C, dtype=jnp.float32)
    diff = idx[:, None] - idx[None, :]
    decay = jnp.where(diff >= 0, jnp.exp(log_g[:, None, None] * jnp.maximum(diff, 0.0)), 0.0)
    scores = jnp.einsum('nbhid,nbhjd->nbhij', qc, kc) * decay
    inner = jnp.einsum('nbhij,nbhjv->nbhiv', scores, vc)
    q_decay = jnp.exp(log_g[:, None] * (idx + 1.0))[:, :, None]
    k_decay = jnp.exp(log_g[:, None] * (C - 1.0 - idx))[:, :, None]
    chunk_decay = jnp.exp(log_g * C)[:, None, None]

    def step(state, xs):
        qi, ki, vi = xs
        cross = jnp.einsum('bhid,bhdv->bhiv', qi, state) * q_decay
        state = state * chunk_decay + jnp.einsum('bhjd,bhjv->bhdv', ki * k_decay, vi)
        return state, cross

    _, cross = lax.scan(step, jnp.zeros((B, Hh, dk, dv), jnp.float32), (qc, kc, vc))
    o = (inner + cross).transpose(1, 0, 3, 2, 4).reshape(B, T, Hh, dv)
    mu = jnp.mean(o, axis=-1, keepdims=True)
    var = jnp.mean(jnp.square(o - mu), axis=-1, keepdims=True)
    o = ((o - mu) * lax.rsqrt(var + EPS)).reshape(B, T, Hh * dv) * gn.astype(jnp.float32)
    y = (jax.nn.silu(g.astype(jnp.float32)) * o).astype(h.dtype)
    return y @ w_out


def nsa_shared_kv(x, kv_norm, w_kv, cmp_pos_k, cmp_pos_v, cmp_w1_k, cmp_w2_k, cmp_w1_v, cmp_w2_v):
    B, T, _ = x.shape
    G, d = NSA_GROUPS, NSA_HD
    hk = rmsnorm(x, kv_norm)
    k_c, v_c, k_s, v_s, k_w, v_w = [a.reshape(B, T, G, d) for a in jnp.split(hk @ w_kv, 6, axis=-1)]
    freqs = partial_rope_freqs()
    k_c = apply_rotary(k_c, freqs)
    k_s = apply_rotary(k_s, freqs)
    k_w = apply_rotary(k_w, freqs)
    n_cmp = (T - L_CMP) // CMP_STRIDE + 1
    idx = np.arange(n_cmp)[:, None] * CMP_STRIDE + np.arange(L_CMP)[None, :]

    def compress(a, pos, w1, w2):
        blocks = a[:, idx] + pos[None, None, :, None, :]
        blocks = blocks.transpose(0, 1, 3, 2, 4).reshape(B, n_cmp, G, L_CMP * d)
        return jax.nn.gelu(blocks @ w1) @ w2

    ck = compress(k_c, cmp_pos_k, cmp_w1_k, cmp_w2_k)
    cv = compress(v_c, cmp_pos_v, cmp_w1_v, cmp_w2_v)
    pad = ((0, 0), (WINDOW, 0), (0, 0), (0, 0))
    return (ck, cv, k_s.transpose(0, 2, 1, 3), v_s.transpose(0, 2, 1, 3),
            jnp.pad(k_w, pad), jnp.pad(v_w, pad))


def nsa_attention(h, w_in, w_out, shared):
    ck, cv, ks_all, vs_all, kw_pad, vw_pad = shared
    B, T, _ = h.shape
    H, G, R, d = NSA_HEADS, NSA_GROUPS, NSA_REP, NSA_HD
    nb = T // Q_BLOCK
    n_cmp = ck.shape[1]
    n_slc = T // L_SLC
    n_sel = min(N_SELECT, n_slc)
    proj = h @ w_in
    q = apply_rotary(proj[..., :H * d].reshape(B, T, H, d), partial_rope_freqs()) * (d ** -0.5)
    gates = jax.nn.sigmoid(proj[..., H * d:].astype(jnp.float32))
    q = q.reshape(B, nb, Q_BLOCK, G, R, d).transpose(1, 0, 2, 3, 4, 5)
    gates = gates.reshape(B, nb, Q_BLOCK, G, R, 3).transpose(1, 0, 2, 3, 4, 5)
    cmp_end = jnp.arange(n_cmp) * CMP_STRIDE + L_CMP - 1
    cs = np.arange(n_cmp)[:, None] * CMP_STRIDE
    ss = np.arange(n_slc)[None, :] * L_SLC
    overlap = np.clip(np.minimum(cs + L_CMP, ss + L_SLC) - np.maximum(cs, ss), 0, None) / CMP_STRIDE
    overlap = jnp.asarray(overlap, dtype=jnp.float32)
    slc_ids = jnp.arange(n_slc)
    gather = jax.vmap(jax.vmap(lambda a, i: a[i]))

    def block(xs):
        bi, qb, gb = xs
        t = bi * Q_BLOCK + jnp.arange(Q_BLOCK)
        s = jnp.einsum('bqgrd,bngd->bgrqn', qb, ck, preferred_element_type=jnp.float32)
        valid = cmp_end[None, :] <= t[:, None]
        p_c = jax.nn.softmax(jnp.where(valid, s, NEG), axis=-1) * jnp.any(valid, axis=-1)[:, None]
        o_c = jnp.einsum('bgrqn,bngd->bqgrd', p_c.astype(cv.dtype), cv)
        imp = jnp.einsum('bgqn,ns->bgqs', jnp.sum(p_c, axis=2), overlap)
        cur = t // L_SLC
        forced = (slc_ids[None, :] == 0) | (slc_ids[None, :] == cur[:, None]) | (slc_ids[None, :] == cur[:, None] - 1)
        causal_blk = slc_ids[None, :] * L_SLC <= t[:, None]
        imp = jnp.where(forced, BIG, jnp.where(causal_blk, imp, NEG))
        _, sel = lax.top_k(imp, n_sel)
        tok = (sel[..., None] * L_SLC + jnp.arange(L_SLC)).reshape(B, G, Q_BLOCK, n_sel * L_SLC)
        ks = gather(ks_all, tok)
        vs = gather(vs_all, tok)
        s = jnp.einsum('bqgrd,bgqkd->bgrqk', qb, ks, preferred_element_type=jnp.float32)
        p_s = jax.nn.softmax(jnp.where(tok[:, :, None] <= t[None, None, None, :, None], s, NEG), axis=-1)
        o_s = jnp.einsum('bgrqk,bgqkd->bqgrd', p_s.astype(vs.dtype), vs)
        kw = lax.dynamic_slice_in_dim(kw_pad, bi * Q_BLOCK, WINDOW + Q_BLOCK, axis=1)
        vw = lax.dynamic_slice_in_dim(vw_pad, bi * Q_BLOCK, WINDOW + Q_BLOCK, axis=1)
        spos = bi * Q_BLOCK - WINDOW + jnp.arange(WINDOW + Q_BLOCK)
        dist = t[:, None] - spos[None, :]
        wmask = (dist >= 0) & (dist < WINDOW) & (spos[None, :] >= 0)
        s = jnp.einsum('bqgrd,bsgd->bgrqs', qb, kw, preferred_element_type=jnp.float32)
        p_w = jax.nn.softmax(jnp.where(wmask, s, NEG), axis=-1)
        o_w = jnp.einsum('bgrqs,bsgd->bqgrd', p_w.astype(vw.dtype), vw)
        o = gb[..., 0:1] * o_c + gb[..., 1:2] * o_s + gb[..., 2:3] * o_w
        return o.astype(h.dtype)

    o = lax.map(block, (jnp.arange(nb), q, gates))
    o = o.transpose(1, 0, 2, 3, 4, 5).reshape(B, T, H * d)
    return o @ w_out


def setup_inputs(seed: int = 0) -> dict:
    key = jax.random.key(seed)
    ks = jax.random.split(key, 20)
    f32 = jnp.float32
    ret_in = 2 * RET_HEADS * RET_DK + 2 * RET_HEADS * RET_DV
    nsa_in = NSA_HEADS * NSA_HD + 3 * NSA_HEADS

    def w(k, shape, fan_in):
        return jax.random.normal(k, shape, f32) * (fan_in ** -0.5)

    def gain(k, shape):
        return 1.0 + 0.05 * jax.random.normal(k, shape, f32)

    return {
        'x': jax.random.normal(ks[0], (BATCH, SEQ, D_MODEL), f32),
        'norms': gain(ks[1], (DEPTH, 4, D_MODEL)),
        'ret_w_in': w(ks[2], (N_A, D_MODEL, ret_in), D_MODEL),
        'ret_gn': gain(ks[3], (N_A, RET_HEADS * RET_DV)),
        'ret_w_out': w(ks[4], (N_A, RET_HEADS * RET_DV, D_MODEL), RET_HEADS * RET_DV),
        'nsa_w_in': w(ks[5], (N_B, D_MODEL, nsa_in), D_MODEL),
        'nsa_w_out': w(ks[6], (N_B, NSA_HEADS * NSA_HD, D_MODEL), NSA_HEADS * NSA_HD),
        'kv_norm': gain(ks[7], (D_MODEL,)),
        'w_kv': w(ks[8], (D_MODEL, 6 * NSA_GROUPS * NSA_HD), D_MODEL),
        'cmp_pos_k': 0.1 * jax.random.normal(ks[9], (L_CMP, NSA_HD), f32),
        'cmp_pos_v': 0.1 * jax.random.normal(ks[10], (L_CMP, NSA_HD), f32),
        'cmp_w1_k': w(ks[11], (L_CMP * NSA_HD, CMP_HIDDEN), L_CMP * NSA_HD),
        'cmp_w2_k': w(ks[12], (CMP_HIDDEN, NSA_HD), CMP_HIDDEN),
        'cmp_w1_v': w(ks[13], (L_CMP * NSA_HD, CMP_HIDDEN), L_CMP * NSA_HD),
        'cmp_w2_v': w(ks[14], (CMP_HIDDEN, NSA_HD), CMP_HIDDEN),
        'ffn_w_in': w(ks[15], (DEPTH, D_MODEL, 2 * D_FF), D_MODEL),
        'ffn_w_out': w(ks[16], (DEPTH, D_FF, D_MODEL), D_FF),
    }


def reference(x, norms, ret_w_in, ret_gn, ret_w_out, nsa_w_in, nsa_w_out, kv_norm, w_kv,
              cmp_pos_k, cmp_pos_v, cmp_w1_k, cmp_w2_k, cmp_w1_v, cmp_w2_v, ffn_w_in, ffn_w_out):
    shared = None
    for layer in range(DEPTH):
        h = rmsnorm(x, norms[layer, 0])
        if layer < N_A:
            m = retention(h, ret_w_in[layer], ret_gn[layer], ret_w_out[layer])
        else:
            m = nsa_attention(h, nsa_w_in[layer - N_A], nsa_w_out[layer - N_A], shared)
        x = x + rmsnorm(m, norms[layer, 1])
        f = swiglu(rmsnorm(x, norms[layer, 2]), ffn_w_in[layer], ffn_w_out[layer])
        x = x + rmsnorm(f, norms[layer, 3])
        if layer == N_A - 1:
            shared = nsa_shared_kv(x, kv_norm, w_kv, cmp_pos_k, cmp_pos_v,
                                   cmp_w1_k, cmp_w2_k, cmp_w1_v, cmp_w2_v)
    return x
```

```python
import functools

import jax
import jax.numpy as jnp
import numpy as np
from jax import lax
from jax.experimental import pallas as pl
from jax.experimental.pallas import tpu as pltpu

BF = jnp.bfloat16
F32 = jnp.float32

D_MODEL = 1024
RET_HEADS = 4
RET_DK = 256
RET_DV = 512
RET_CHUNK = 128
RET_ROT_BASE = 10000.0
NSA_HEADS = 16
NSA_GROUPS = 4
NSA_HD = 64
NSA_REP = 4
L_CMP = 32
CMP_STRIDE = 16
L_SLC = 64
N_SELECT = 16
WINDOW = 512
CMP_HIDDEN = 256
Q_BLOCK = 128
ROPE_THETA = 500000.0
ROPE_DIMS = 16
ROPE_HALF = ROPE_DIMS // 2
D_FF = 2816
EPS = 1e-6
NEG = -1e30
BIG = 1e9
TAKEN = -3e38

LANES = 128
SEL_KEYS = 256
VMEM_LIMIT = 48 * 1024 * 1024


def _cparams(n_axes):
    return pltpu.CompilerParams(dimension_semantics=("arbitrary",) * n_axes,
                                vmem_limit_bytes=VMEM_LIMIT)


def _rms_scale(x):
    return x * lax.rsqrt(jnp.mean(x * x, axis=-1, keepdims=True) + EPS)


def _dot(a, b):
    return jnp.dot(a, b, preferred_element_type=F32)


def _dot_nt(a, b):
    return lax.dot_general(a, b, (((1,), (1,)), ((), ())), preferred_element_type=F32)


def _dot_tn(a, b):
    return lax.dot_general(a, b, (((0,), (0,)), ((), ())), preferred_element_type=F32)


def _ret_proj_kernel(x_ref, g_ref, w_ref, cos_ref, sin_ref, o_ref, hn_ref, *, n_q_tiles, n_rot_tiles):
    j = pl.program_id(1)

    @pl.when(j == 0)
    def _():
        hn_ref[...] = (_rms_scale(x_ref[...]) * g_ref[...]).astype(BF)

    acc = _dot(hn_ref[...], w_ref[...])

    @pl.when(j < n_rot_tiles)
    def _():
        cos = cos_ref[...]
        sin = sin_ref[...]
        scale = jnp.where(j >= n_q_tiles, RET_DK ** -0.5, 1.0).astype(F32)
        half = RET_DK // 2
        for h in range(acc.shape[1] // RET_DK):
            x1 = acc[:, h * RET_DK:h * RET_DK + half]
            x2 = acc[:, h * RET_DK + half:(h + 1) * RET_DK]
            o_ref[:, h * RET_DK:h * RET_DK + half] = ((x1 * cos - x2 * sin) * scale).astype(BF)
            o_ref[:, h * RET_DK + half:(h + 1) * RET_DK] = ((x1 * sin + x2 * cos) * scale).astype(BF)

    @pl.when(j >= n_rot_tiles)
    def _():
        o_ref[...] = acc.astype(BF)


def _ret_proj(x2d, g, w, cos, sin, *, seq, tm=512, tn=512):
    n, d = x2d.shape
    nout = w.shape[1]
    n_q_tiles = (RET_HEADS * RET_DK) // tn
    tpos = seq // tm
    kern = functools.partial(_ret_proj_kernel, n_q_tiles=n_q_tiles, n_rot_tiles=2 * n_q_tiles)
    return pl.pallas_call(
        kern,
        grid=(n // tm, nout // tn),
        in_specs=[
            pl.BlockSpec((tm, d), lambda i, j: (i, 0)),
            pl.BlockSpec((1, d), lambda i, j: (0, 0)),
            pl.BlockSpec((d, tn), lambda i, j: (0, j)),
            pl.BlockSpec((tm, RET_DK // 2), lambda i, j: (i % tpos, 0)),
            pl.BlockSpec((tm, RET_DK // 2), lambda i, j: (i % tpos, 0)),
        ],
        out_specs=pl.BlockSpec((tm, tn), lambda i, j: (i, j)),
        out_shape=jax.ShapeDtypeStruct((n, nout), BF),
        scratch_shapes=[pltpu.VMEM((tm, d), BF)],
        compiler_params=_cparams(2),
        name="ret_proj",
    )(x2d, g, w, cos, sin)


def _retention_kernel(q_ref, k_ref, v_ref, g_ref, decay_ref, qd_ref, kd_ref, cd_ref, gn_ref, o_ref, state_ref):
    c = pl.program_id(2)

    @pl.when(c == 0)
    def _():
        state_ref[...] = jnp.zeros_like(state_ref)

    q = q_ref[0]
    k = k_ref[0]
    v = v_ref[0]
    scores = _dot_nt(q, k) * decay_ref[0]
    inner = _dot(scores.astype(BF), v)
    state = state_ref[...]
    cross = _dot(q, state.astype(BF)) * qd_ref[0]
    kk = (k.astype(F32) * kd_ref[0]).astype(BF)
    state_ref[...] = state * cd_ref[0] + _dot_tn(kk, v)
    o = inner + cross
    mu = jnp.mean(o, axis=-1, keepdims=True)
    dlt = o - mu
    var = jnp.mean(dlt * dlt, axis=-1, keepdims=True)
    on = dlt * lax.rsqrt(var + EPS) * gn_ref[...]
    gate = g_ref[0].astype(F32)
    o_ref[0] = (gate * jax.nn.sigmoid(gate) * on).astype(BF)


def _retention(proj, decay, qd, kd, cd, gn, *, batch, seq):
    c = RET_CHUNK
    nq = RET_HEADS
    proj3 = proj.reshape(batch, seq, proj.shape[-1])
    return pl.pallas_call(
        _retention_kernel,
        grid=(batch, RET_HEADS, seq // c),
        in_specs=[
            pl.BlockSpec((1, c, RET_DK), lambda b, h, i: (b, i, h)),
            pl.BlockSpec((1, c, RET_DK), lambda b, h, i: (b, i, nq + h)),
            pl.BlockSpec((1, c, RET_DV), lambda b, h, i: (b, i, nq + h)),
            pl.BlockSpec((1, c, RET_DV), lambda b, h, i: (b, i, 2 * nq + h)),
            pl.BlockSpec((1, c, c), lambda b, h, i: (h, 0, 0)),
            pl.BlockSpec((1, c, RET_DV), lambda b, h, i: (h, 0, 0)),
            pl.BlockSpec((1, c, RET_DK), lambda b, h, i: (h, 0, 0)),
            pl.BlockSpec((1, 1, RET_DV), lambda b, h, i: (h, 0, 0)),
            pl.BlockSpec((1, RET_DV), lambda b, h, i: (0, h)),
        ],
        out_specs=pl.BlockSpec((1, c, RET_DV), lambda b, h, i: (b, i, h)),
        out_shape=jax.ShapeDtypeStruct((batch, seq, RET_HEADS * RET_DV), BF),
        scratch_shapes=[pltpu.VMEM((RET_DK, RET_DV), F32)],
        compiler_params=_cparams(3),
        name="retention",
    )(proj3, proj3, proj3, proj3, decay, qd, kd, cd, gn)


def _out_proj_kernel(y_ref, w_ref, g_ref, x_ref, o_ref):
    m = _dot(y_ref[...], w_ref[...])
    o_ref[...] = x_ref[...] + _rms_scale(m) * g_ref[...]


def _out_proj_res(y2d, w, g, x2d, *, tm=512):
    n, kdim = y2d.shape
    d = w.shape[1]
    return pl.pallas_call(
        _out_proj_kernel,
        grid=(n // tm,),
        in_specs=[
            pl.BlockSpec((tm, kdim), lambda i: (i, 0)),
            pl.BlockSpec((kdim, d), lambda i: (0, 0)),
            pl.BlockSpec((1, d), lambda i: (0, 0)),
            pl.BlockSpec((tm, d), lambda i: (i, 0)),
        ],
        out_specs=pl.BlockSpec((tm, d), lambda i: (i, 0)),
        out_shape=jax.ShapeDtypeStruct((n, d), F32),
        compiler_params=_cparams(1),
        name="out_proj_res",
    )(y2d, w, g, x2d)


def _ffn_kernel(x_ref, g_in_ref, wg_ref, wu_ref, wo_ref, g_out_ref, o_ref, hn_ref, acc_ref, *, n_ff):
    f = pl.program_id(1)

    @pl.when(f == 0)
    def _():
        hn_ref[...] = (_rms_scale(x_ref[...]) * g_in_ref[...]).astype(BF)
        acc_ref[...] = jnp.zeros_like(acc_ref)

    h = hn_ref[...]
    gate = _dot(h, wg_ref[...])
    up = _dot(h, wu_ref[...])
    act = (gate * jax.nn.sigmoid(gate) * up).astype(BF)
    acc_ref[...] += _dot(act, wo_ref[...])

    @pl.when(f == n_ff - 1)
    def _():
        o_ref[...] = x_ref[...] + _rms_scale(acc_ref[...]) * g_out_ref[...]


def _ffn(x2d, g_in, w_in, w_out, g_out, *, tm=512, tf=1408):
    n, d = x2d.shape
    n_ff = D_FF // tf
    kern = functools.partial(_ffn_kernel, n_ff=n_ff)
    return pl.pallas_call(
        kern,
        grid=(n // tm, n_ff),
        in_specs=[
            pl.BlockSpec((tm, d), lambda i, f: (i, 0)),
            pl.BlockSpec((1, d), lambda i, f: (0, 0)),
            pl.BlockSpec((d, tf), lambda i, f: (0, f)),
            pl.BlockSpec((d, tf), lambda i, f: (0, n_ff + f)),
            pl.BlockSpec((tf, d), lambda i, f: (f, 0)),
            pl.BlockSpec((1, d), lambda i, f: (0, 0)),
        ],
        out_specs=pl.BlockSpec((tm, d), lambda i, f: (i, 0)),
        out_shape=jax.ShapeDtypeStruct((n, d), F32),
        scratch_shapes=[pltpu.VMEM((tm, d), BF), pltpu.VMEM((tm, d), F32)],
        compiler_params=_cparams(2),
        name="ffn",
    )(x2d, g_in, w_in, w_in, w_out, g_out)


def _kv_proj_kernel(x_ref, g_ref, wn_ref, wt_ref, c_ref, s1_ref, s2_ref,
                    kc_ref, vc_ref, ks_ref, kw_ref, vst_ref, vwt_ref):
    hn = (_rms_scale(x_ref[...]) * g_ref[...]).astype(BF)
    tm = hn.shape[0]
    gw = NSA_GROUPS * NSA_HD
    rn = _dot(hn, wn_ref[...])
    cc, s1, s2 = c_ref[...], s1_ref[...], s2_ref[...]
    outs = (kc_ref, vc_ref, ks_ref, kw_ref)
    rotate = (True, False, True, True)
    for sec in range(4):
        for pair in range(gw // LANES):
            col = sec * gw + pair * LANES
            x = rn[:, col:col + LANES]
            if rotate[sec]:
                x = (x * cc + pltpu.roll(x, ROPE_HALF, axis=1) * s1
                     + pltpu.roll(x, LANES - ROPE_HALF, axis=1) * s2)
            xb = x.astype(BF)
            outs[sec][0, 2 * pair] = xb[:, :NSA_HD]
            outs[sec][0, 2 * pair + 1] = xb[:, NSA_HD:]
    rt = _dot_nt(wt_ref[...], hn).astype(BF)
    for g in range(NSA_GROUPS):
        for cb in range(tm // SEL_KEYS):
            vst_ref[0, g, cb] = rt[g * NSA_HD:(g + 1) * NSA_HD, cb * SEL_KEYS:(cb + 1) * SEL_KEYS]
        for cb in range(tm // LANES):
            vwt_ref[0, g, cb] = rt[gw + g * NSA_HD:gw + (g + 1) * NSA_HD, cb * LANES:(cb + 1) * LANES]


def _kv_proj(x2d, g, wn, wt, ctab, s1tab, s2tab, *, batch, seq, tm=512):
    n, d = x2d.shape
    tpos = seq // tm
    gdim, hd = NSA_GROUPS, NSA_HD
    normal = jax.ShapeDtypeStruct((batch, gdim, seq, hd), BF)
    normal_spec = pl.BlockSpec((1, gdim, tm, hd), lambda i: (i // tpos, 0, i % tpos, 0))
    tab_spec = pl.BlockSpec((tm, LANES), lambda i: (i % tpos, 0))
    return pl.pallas_call(
        _kv_proj_kernel,
        grid=(n // tm,),
        in_specs=[
            pl.BlockSpec((tm, d), lambda i: (i, 0)),
            pl.BlockSpec((1, d), lambda i: (0, 0)),
            pl.BlockSpec(wn.shape, lambda i: (0, 0)),
            pl.BlockSpec(wt.shape, lambda i: (0, 0)),
            tab_spec, tab_spec, tab_spec,
        ],
        out_specs=[
            normal_spec, normal_spec, normal_spec, normal_spec,
            pl.BlockSpec((1, gdim, tm // SEL_KEYS, hd, SEL_KEYS), lambda i: (i // tpos, 0, i % tpos, 0, 0)),
            pl.BlockSpec((1, gdim, tm // LANES, hd, LANES), lambda i: (i // tpos, 0, i % tpos, 0, 0)),
        ],
        out_shape=[
            normal, normal, normal, normal,
            jax.ShapeDtypeStruct((batch, gdim, seq // SEL_KEYS, hd, SEL_KEYS), BF),
            jax.ShapeDtypeStruct((batch, gdim, seq // LANES, hd, LANES), BF),
        ],
        compiler_params=_cparams(1),
        name="kv_proj",
    )(x2d, g, wn, wt, ctab, s1tab, s2tab)


def _gelu_tanh(x):
    return 0.5 * x * (1.0 + jnp.tanh(0.7978845608028654 * (x + 0.044715 * (x * x * x))))


def _compress_kernel(x_ref, pos_ref, w1_ref, w1cat_ref, w2_ref, o_ref, *, transposed_out):
    half = CMP_STRIDE * NSA_HD
    pb = _dot(pos_ref[...], w1_ref[...])[0:1]
    first, second = [], []
    for k in range(4):
        res = _dot(x_ref[0, 0, :, k * half:(k + 1) * half], w1cat_ref[...])
        first.append(res[:, :CMP_HIDDEN])
        second.append(res[:, CMP_HIDDEN:])
    rows = first[0].shape[0]
    for k in range(4):
        nxt = second[k + 1] if k < 3 else pltpu.roll(second[0], rows - 1, axis=0)
        hid = _gelu_tanh(first[k] + nxt + pb).astype(BF)
        if transposed_out:
            o_ref[0, 0, :, k * rows:(k + 1) * rows] = _dot_nt(w2_ref[...], hid).astype(BF)
        else:
            o_ref[0, 0, k * rows:(k + 1) * rows, :] = _dot(hid, w2_ref[...]).astype(BF)


def _compress(x4, pos8, w1, w1cat, w2, *, transposed_out):
    batch, gdim, rows, feat = x4.shape
    n_rows = 4 * rows
    if transposed_out:
        out_shape = jax.ShapeDtypeStruct((batch, gdim, NSA_HD, n_rows), BF)
        out_spec = pl.BlockSpec((1, 1, NSA_HD, n_rows), lambda b, g: (b, g, 0, 0))
    else:
        out_shape = jax.ShapeDtypeStruct((batch, gdim, n_rows, NSA_HD), BF)
        out_spec = pl.BlockSpec((1, 1, n_rows, NSA_HD), lambda b, g: (b, g, 0, 0))
    kern = functools.partial(_compress_kernel, transposed_out=transposed_out)
    return pl.pallas_call(
        kern,
        grid=(batch, gdim),
        in_specs=[
            pl.BlockSpec((1, 1, rows, feat), lambda b, g: (b, g, 0, 0)),
            pl.BlockSpec(pos8.shape, lambda b, g: (0, 0)),
            pl.BlockSpec(w1.shape, lambda b, g: (0, 0)),
            pl.BlockSpec(w1cat.shape, lambda b, g: (0, 0)),
            pl.BlockSpec(w2.shape, lambda b, g: (0, 0)),
        ],
        out_specs=out_spec,
        out_shape=out_shape,
        compiler_params=_cparams(2),
        name="compress_v" if transposed_out else "compress_k",
    )(x4, pos8, w1, w1cat, w2)


def _nsa_q_proj_kernel(x_ref, g_ref, wq_ref, wg_ref, cos_ref, sin_ref, q_ref, gate_ref):
    hn = (_rms_scale(x_ref[...]) * g_ref[...]).astype(BF)
    tm = hn.shape[0]
    qt = _dot_nt(wq_ref[...], hn) * (NSA_HD ** -0.5)
    cos, sin = cos_ref[...], sin_ref[...]
    for h in range(NSA_HEADS):
        g, r = divmod(h, NSA_REP)
        base = h * NSA_HD
        x1 = qt[base:base + ROPE_HALF]
        x2 = qt[base + ROPE_HALF:base + ROPE_DIMS]
        head = jnp.concatenate(
            [x1 * cos - x2 * sin, x1 * sin + x2 * cos, qt[base + ROPE_DIMS:base + NSA_HD]], axis=0).astype(BF)
        for blk in range(tm // Q_BLOCK):
            col = blk * NSA_REP * Q_BLOCK + r * Q_BLOCK
            q_ref[0, g, :, col:col + Q_BLOCK] = head[:, blk * Q_BLOCK:(blk + 1) * Q_BLOCK]
    gt = jax.nn.sigmoid(_dot_nt(wg_ref[...], hn))
    for g in range(NSA_GROUPS):
        gate_ref[0, g] = gt[g * 16:(g + 1) * 16]


def _nsa_q_proj(x2d, g, wqt, wgt, cos_t, sin_t, *, batch, seq, tm=512):
    n, d = x2d.shape
    tpos = seq // tm
    qcols = NSA_REP * Q_BLOCK
    return pl.pallas_call(
        _nsa_q_proj_kernel,
        grid=(n // tm,),
        in_specs=[
            pl.BlockSpec((tm, d), lambda i: (i, 0)),
            pl.BlockSpec((1, d), lambda i: (0, 0)),
            pl.BlockSpec(wqt.shape, lambda i: (0, 0)),
            pl.BlockSpec(wgt.shape, lambda i: (0, 0)),
            pl.BlockSpec((ROPE_HALF, tm), lambda i: (0, i % tpos)),
            pl.BlockSpec((ROPE_HALF, tm), lambda i: (0, i % tpos)),
        ],
        out_specs=[
            pl.BlockSpec((1, NSA_GROUPS, NSA_HD, (tm // Q_BLOCK) * qcols), lambda i: (i // tpos, 0, 0, i % tpos)),
            pl.BlockSpec((1, NSA_GROUPS, 16, tm), lambda i: (i // tpos, 0, 0, i % tpos)),
        ],
        out_shape=[
            jax.ShapeDtypeStruct((batch, NSA_GROUPS, NSA_HD, (seq // Q_BLOCK) * qcols), BF),
            jax.ShapeDtypeStruct((batch, NSA_GROUPS, 16, seq), F32),
        ],
        compiler_params=_cparams(1),
        name="nsa_q_proj",
    )(x2d, g, wqt, wgt, cos_t, sin_t)


def _softmax_cols(s):
    m = jnp.max(s, axis=0, keepdims=True)
    p = jnp.exp(s - m)
    return p, jnp.sum(p, axis=0, keepdims=True)


def _nsa_attn_kernel(q_ref, gate_ref, ck_ref, cvt_ref, ks_ref, vst_ref, kw_ref, vwt_ref, o_ref,
                     mt_ref, m_ref, l_ref, acc_ref):
    bi = pl.program_id(2)
    qcols = NSA_REP * Q_BLOCK
    qt = q_ref[0, 0]
    lane = lax.broadcasted_iota(jnp.int32, (1, qcols), 1)
    t = bi * Q_BLOCK + (lane & (Q_BLOCK - 1))

    n_rows = ck_ref.shape[2]
    n_slc = n_rows // 4
    s = _dot(ck_ref[0, 0], qt)
    rho = lax.broadcasted_iota(jnp.int32, (n_rows, 1), 0)
    slc_bits = n_slc.bit_length() - 1
    n_idx = 4 * (rho & (n_slc - 1)) + (rho >> slc_bits)
    valid = (n_idx * CMP_STRIDE + (L_CMP - 1)) <= t
    p, l = _softmax_cols(jnp.where(valid, s, NEG))
    pc = p * jnp.where(t >= L_CMP - 1, 1.0 / l, 0.0)
    oc = _dot(cvt_ref[0, 0], pc.astype(BF))

    ps = pc[:, 0:Q_BLOCK]
    for r in range(1, NSA_REP):
        ps = ps + pc[:, r * Q_BLOCK:(r + 1) * Q_BLOCK]
    p0, p1, p2, p3 = (ps[k * n_slc:(k + 1) * n_slc] for k in range(4))
    j_io = lax.broadcasted_iota(jnp.int32, (n_slc, Q_BLOCK), 0)
    p3_prev = jnp.where(j_io == 0, 0.0, pltpu.roll(p3, 1, axis=0))
    imp = p3_prev + 2.0 * p0 + 2.0 * p1 + 2.0 * p2 + p3

    tq = bi * Q_BLOCK + lax.broadcasted_iota(jnp.int32, (1, Q_BLOCK), 1)
    cur = tq >> (L_SLC.bit_length() - 1)
    forced = (j_io == 0) | (j_io == cur) | (j_io == cur - 1)
    val = jnp.where(forced, BIG, jnp.where(j_io <= cur, imp, NEG))
    j_f = j_io.astype(F32)
    mt = jnp.zeros((n_slc, Q_BLOCK), F32)
    for _ in range(min(N_SELECT, n_slc)):
        mx = jnp.max(val, axis=0, keepdims=True)
        idx = jnp.min(jnp.where(val == mx, j_f, float(n_slc)), axis=0, keepdims=True)
        hit = j_f == idx
        mt = jnp.where(hit, 1.0, mt)
        val = jnp.where(hit, TAKEN, val)
    for r in range(NSA_REP):
        mt_ref[:, r * Q_BLOCK:(r + 1) * Q_BLOCK] = mt

    m_ref[...] = jnp.full_like(m_ref, NEG)
    l_ref[...] = jnp.zeros_like(l_ref)
    acc_ref[...] = jnp.zeros_like(acc_ref)
    blocks_per_step = SEL_KEYS // L_SLC
    row_io = lax.broadcasted_iota(jnp.int32, (SEL_KEYS, 1), 0)

    def sel_step(c, carry):
        k = ks_ref[0, 0, pl.ds(pl.multiple_of(c * SEL_KEYS, SEL_KEYS), SEL_KEYS), :]
        sc = _dot(k, qt)
        sel = jnp.concatenate(
            [jnp.broadcast_to(mt_ref[pl.ds(c * blocks_per_step + jj, 1), :], (L_SLC, qcols))
             for jj in range(blocks_per_step)], axis=0)
        allowed = (sel > 0.5) & ((c * SEL_KEYS + row_io) <= t)
        sc = jnp.where(allowed, sc, NEG)
        m_old = m_ref[...]
        m_new = jnp.maximum(m_old, jnp.max(sc, axis=0, keepdims=True))
        alpha = jnp.exp(m_old - m_new)
        pr = jnp.exp(sc - m_new)
        l_ref[...] = alpha * l_ref[...] + jnp.sum(pr, axis=0, keepdims=True)
        acc_ref[...] = alpha * acc_ref[...] + _dot(vst_ref[0, 0, c], pr.astype(BF))
        m_ref[...] = m_new
        return carry

    n_steps = ((bi + 1) * Q_BLOCK + SEL_KEYS - 1) >> (SEL_KEYS.bit_length() - 1)
    lax.fori_loop(0, n_steps, sel_step, 0)
    osel = acc_ref[...] * (1.0 / l_ref[...])

    w_keys = WINDOW + Q_BLOCK
    wb = jnp.maximum(bi - WINDOW // Q_BLOCK, 0)
    start = pl.multiple_of(wb * Q_BLOCK, Q_BLOCK)
    kw = kw_ref[0, 0, pl.ds(start, w_keys), :]
    sw = _dot(kw, qt)
    dist = t - (start + lax.broadcasted_iota(jnp.int32, (w_keys, 1), 0))
    pw, lw = _softmax_cols(jnp.where((dist >= 0) & (dist < WINDOW), sw, NEG))
    vw = jnp.concatenate([vwt_ref[0, 0, wb + w] for w in range(w_keys // LANES)], axis=1)
    ow = _dot(vw, pw.astype(BF)) * (1.0 / lw)

    heads = []
    for r in range(NSA_REP):
        sl = slice(r * Q_BLOCK, (r + 1) * Q_BLOCK)
        gc = gate_ref[0, 0, 0 * NSA_REP + r:0 * NSA_REP + r + 1, :]
        gs = gate_ref[0, 0, 1 * NSA_REP + r:1 * NSA_REP + r + 1, :]
        gw = gate_ref[0, 0, 2 * NSA_REP + r:2 * NSA_REP + r + 1, :]
        heads.append(gc * oc[:, sl] + gs * osel[:, sl] + gw * ow[:, sl])
    o_ref[0] = jnp.concatenate(heads, axis=0).T.astype(BF)


def _nsa_attn(qt, gates, ck, cvt, ks, vst, kw, vwt, *, batch, seq):
    nb = seq // Q_BLOCK
    qcols = NSA_REP * Q_BLOCK
    n_cmp_rows = ck.shape[2]
    hd = NSA_HD
    full = lambda shape: pl.BlockSpec((1, 1) + tuple(shape[2:]), lambda b, g, i: (b, g) + (0,) * (len(shape) - 2))
    return pl.pallas_call(
        _nsa_attn_kernel,
        grid=(batch, NSA_GROUPS, nb),
        in_specs=[
            pl.BlockSpec((1, 1, hd, qcols), lambda b, g, i: (b, g, 0, i)),
            pl.BlockSpec((1, 1, 16, Q_BLOCK), lambda b, g, i: (b, g, 0, i)),
            full(ck.shape), full(cvt.shape), full(ks.shape), full(vst.shape), full(kw.shape), full(vwt.shape),
        ],
        out_specs=pl.BlockSpec((1, Q_BLOCK, NSA_REP * hd), lambda b, g, i: (b, i, g)),
        out_shape=jax.ShapeDtypeStruct((batch, seq, NSA_HEADS * hd), BF),
        scratch_shapes=[
            pltpu.VMEM((n_cmp_rows // 4, qcols), F32),
            pltpu.VMEM((1, qcols), F32),
            pltpu.VMEM((1, qcols), F32),
            pltpu.VMEM((hd, qcols), F32),
        ],
        compiler_params=_cparams(3),
        name="nsa_attn",
    )(qt, gates, ck, cvt, ks, vst, kw, vwt)


def _retention_tables(seq):
    c = RET_CHUNK
    freqs = 1.0 / (RET_ROT_BASE ** jnp.linspace(0.0, 1.0, RET_DK // 2, dtype=F32))
    ang = jnp.arange(seq, dtype=F32)[:, None] * freqs[None, :]
    log_g = jnp.log1p(-jnp.exp2(-5.0 - jnp.arange(RET_HEADS, dtype=F32)))
    idx = jnp.arange(c, dtype=F32)
    diff = idx[:, None] - idx[None, :]
    decay = jnp.where(diff >= 0, jnp.exp(log_g[:, None, None] * jnp.maximum(diff, 0.0)), 0.0)
    qd = jnp.exp(log_g[:, None] * (idx + 1.0))[:, :, None]
    kd = jnp.exp(log_g[:, None] * (c - 1.0 - idx))[:, :, None]
    cd = jnp.exp(log_g * c)[:, None, None]
    return (jnp.cos(ang), jnp.sin(ang), decay,
            jnp.broadcast_to(qd, (RET_HEADS, c, RET_DV)),
            jnp.broadcast_to(kd, (RET_HEADS, c, RET_DK)),
            jnp.broadcast_to(cd, (RET_HEADS, 1, RET_DV)))


def _rope_tables(seq):
    freqs = ROPE_THETA ** (-jnp.arange(0, ROPE_DIMS, 2, dtype=F32) / ROPE_DIMS)
    ang = jnp.arange(seq, dtype=F32)[:, None] * freqs[None, :]
    cos, sin = jnp.cos(ang), jnp.sin(ang)
    ones = jnp.ones((seq, NSA_HD - ROPE_DIMS), F32)
    zeros8 = jnp.zeros((seq, ROPE_HALF), F32)
    zeros_rest = jnp.zeros((seq, NSA_HD - ROPE_DIMS), F32)
    ctab = jnp.concatenate([cos, cos, ones], axis=1)
    s1 = jnp.concatenate([zeros8, sin, zeros_rest], axis=1)
    s2 = jnp.concatenate([-sin, zeros8, zeros_rest], axis=1)
    two = lambda a: jnp.concatenate([a, a], axis=1)
    return two(ctab), two(s1), two(s2), cos.T, sin.T


def kernel(x, norms, ret_w_in, ret_gn, ret_w_out, nsa_w_in, nsa_w_out, kv_norm, w_kv, cmp_pos_k, cmp_pos_v,
           cmp_w1_k, cmp_w2_k, cmp_w1_v, cmp_w2_v, ffn_w_in, ffn_w_out):
    batch, seq, d = x.shape
    n = batch * seq
    x2d = x.reshape(n, d)
    norm = lambda layer, k: norms[layer, k][None, :]

    cos_r, sin_r, decay, qd, kd, cd = _retention_tables(seq)
    proj = _ret_proj(x2d, norm(0, 0), ret_w_in[0].astype(BF), cos_r, sin_r, seq=seq)
    y = _retention(proj, decay, qd, kd, cd, ret_gn[0][None, :], batch=batch, seq=seq)
    x2d = _out_proj_res(y.reshape(n, -1), ret_w_out[0].astype(BF), norm(0, 1), x2d)
    x2d = _ffn(x2d, norm(0, 2), ffn_w_in[0].astype(BF), ffn_w_out[0].astype(BF), norm(0, 3))

    gw = NSA_GROUPS * NSA_HD
    sec = lambda s: w_kv[:, s * gw:(s + 1) * gw]
    wn = jnp.concatenate([sec(0), sec(1), sec(2), sec(4)], axis=1).astype(BF)
    wt = jnp.concatenate([sec(3), sec(5)], axis=1).T.astype(BF)
    ctab, s1tab, s2tab, cos_t, sin_t = _rope_tables(seq)
    k_cmp, v_cmp, k_slc, k_win, v_slc_t, v_win_t = _kv_proj(
        x2d, kv_norm[None, :], wn, wt, ctab, s1tab, s2tab, batch=batch, seq=seq)

    half = CMP_STRIDE * NSA_HD
    rows = seq // (4 * CMP_STRIDE)

    def compress(a, pos, w1, w2, transposed_out):
        x4 = a.reshape(batch, NSA_GROUPS, rows, 4 * half)
        pos8 = jnp.broadcast_to(pos.reshape(1, L_CMP * NSA_HD), (8, L_CMP * NSA_HD)).astype(BF)
        w1cat = jnp.concatenate([w1[:half], w1[half:]], axis=1).astype(BF)
        w2p = (w2.T if transposed_out else w2).astype(BF)
        return _compress(x4, pos8, w1.astype(BF), w1cat, w2p, transposed_out=transposed_out)

    ck = compress(k_cmp, cmp_pos_k, cmp_w1_k, cmp_w2_k, False)
    cvt = compress(v_cmp, cmp_pos_v, cmp_w1_v, cmp_w2_v, True)

    hq = NSA_HEADS * NSA_HD
    wqt = nsa_w_in[0][:, :hq].T.astype(BF)
    wg = nsa_w_in[0][:, hq:].reshape(d, NSA_GROUPS, NSA_REP, 3).transpose(0, 1, 3, 2)
    wg = jnp.pad(wg.reshape(d, NSA_GROUPS, 3 * NSA_REP), ((0, 0), (0, 0), (0, 16 - 3 * NSA_REP)))
    wgt = wg.reshape(d, NSA_GROUPS * 16).T.astype(BF)
    qt, gates = _nsa_q_proj(x2d, norm(1, 0), wqt, wgt, cos_t, sin_t, batch=batch, seq=seq)
    o = _nsa_attn(qt, gates, ck, cvt, k_slc, v_slc_t, k_win, v_win_t, batch=batch, seq=seq)
    x2d = _out_proj_res(o.reshape(n, -1), nsa_w_out[0].astype(BF), norm(1, 1), x2d)
    x2d = _ffn(x2d, norm(1, 2), ffn_w_in[1].astype(BF), ffn_w_out[1].astype(BF), norm(1, 3))
    return x2d.reshape(batch, seq, d)
```

```python
import functools

import jax
import jax.numpy as jnp
import numpy as np
from jax import lax
from jax.experimental import pallas as pl
from jax.experimental.pallas import tpu as pltpu

BF = jnp.bfloat16
F32 = jnp.float32

D_MODEL = 1024
RET_HEADS = 4
RET_DK = 256
RET_DV = 512
RET_CHUNK = 128
RET_ROT_BASE = 10000.0
NSA_HEADS = 16
NSA_GROUPS = 4
NSA_HD = 64
NSA_REP = 4
L_CMP = 32
CMP_STRIDE = 16
L_SLC = 64
N_SELECT = 16
WINDOW = 512
CMP_HIDDEN = 256
Q_BLOCK = 128
ROPE_THETA = 500000.0
ROPE_DIMS = 16
ROPE_HALF = ROPE_DIMS // 2
D_FF = 2816
EPS = 1e-6
NEG = -1e30
BIG = 1e9
TAKEN = -3e38

LANES = 128
SEL_KEYS = 512
SEL_BLOCKS = SEL_KEYS // L_SLC
SEL_VROWS = NSA_HD + 16
LOG2E = 1.4426950408889634
VMEM_LIMIT = 48 * 1024 * 1024


def _cparams(n_axes):
    return pltpu.CompilerParams(dimension_semantics=("arbitrary",) * n_axes,
                                vmem_limit_bytes=VMEM_LIMIT)


def _rms_scale(x):
    return x * lax.rsqrt(jnp.mean(x * x, axis=-1, keepdims=True) + EPS)


def _dot(a, b):
    return jnp.dot(a, b, preferred_element_type=F32)


def _dot_nt(a, b):
    return lax.dot_general(a, b, (((1,), (1,)), ((), ())), preferred_element_type=F32)


def _dot_tn(a, b):
    return lax.dot_general(a, b, (((0,), (0,)), ((), ())), preferred_element_type=F32)


def _ret_proj_kernel(x_ref, g_ref, w_ref, cos_ref, sin_ref, o_ref, hn_ref, *, n_q_tiles, n_rot_tiles):
    j = pl.program_id(1)

    @pl.when(j == 0)
    def _():
        hn_ref[...] = (_rms_scale(x_ref[...]) * g_ref[...]).astype(BF)

    acc = _dot(hn_ref[...], w_ref[...])

    @pl.when(j < n_rot_tiles)
    def _():
        cos = cos_ref[...]
        sin = sin_ref[...]
        scale = jnp.where(j >= n_q_tiles, RET_DK ** -0.5, 1.0).astype(F32)
        half = RET_DK // 2
        for h in range(acc.shape[1] // RET_DK):
            x1 = acc[:, h * RET_DK:h * RET_DK + half]
            x2 = acc[:, h * RET_DK + half:(h + 1) * RET_DK]
            o_ref[:, h * RET_DK:h * RET_DK + half] = ((x1 * cos - x2 * sin) * scale).astype(BF)
            o_ref[:, h * RET_DK + half:(h + 1) * RET_DK] = ((x1 * sin + x2 * cos) * scale).astype(BF)

    @pl.when(j >= n_rot_tiles)
    def _():
        o_ref[...] = acc.astype(BF)


def _ret_proj(x2d, g, w, cos, sin, *, seq, tm=512, tn=512):
    n, d = x2d.shape
    nout = w.shape[1]
    n_q_tiles = (RET_HEADS * RET_DK) // tn
    tpos = seq // tm
    kern = functools.partial(_ret_proj_kernel, n_q_tiles=n_q_tiles, n_rot_tiles=2 * n_q_tiles)
    return pl.pallas_call(
        kern,
        grid=(n // tm, nout // tn),
        in_specs=[
            pl.BlockSpec((tm, d), lambda i, j: (i, 0)),
            pl.BlockSpec((1, d), lambda i, j: (0, 0)),
            pl.BlockSpec((d, tn), lambda i, j: (0, j)),
            pl.BlockSpec((tm, RET_DK // 2), lambda i, j: (i % tpos, 0)),
            pl.BlockSpec((tm, RET_DK // 2), lambda i, j: (i % tpos, 0)),
        ],
        out_specs=pl.BlockSpec((tm, tn), lambda i, j: (i, j)),
        out_shape=jax.ShapeDtypeStruct((n, nout), BF),
        scratch_shapes=[pltpu.VMEM((tm, d), BF)],
        compiler_params=_cparams(2),
        name="ret_proj",
    )(x2d, g, w, cos, sin)


def _retention_kernel(q_ref, k_ref, v_ref, g_ref, decay_ref, qd_ref, kd_ref, cd_ref, gn_ref, o_ref, state_ref):
    c = pl.program_id(2)

    @pl.when(c == 0)
    def _():
        state_ref[...] = jnp.zeros_like(state_ref)

    q = q_ref[0]
    k = k_ref[0]
    v = v_ref[0]
    scores = _dot_nt(q, k) * decay_ref[0]
    inner = _dot(scores.astype(BF), v)
    state = state_ref[...]
    cross = _dot(q, state.astype(BF)) * qd_ref[0]
    kk = (k.astype(F32) * kd_ref[0]).astype(BF)
    state_ref[...] = state * cd_ref[0] + _dot_tn(kk, v)
    o = inner + cross
    mu = jnp.mean(o, axis=-1, keepdims=True)
    dlt = o - mu
    var = jnp.mean(dlt * dlt, axis=-1, keepdims=True)
    on = dlt * lax.rsqrt(var + EPS) * gn_ref[...]
    gate = g_ref[0].astype(F32)
    o_ref[0] = (gate * jax.nn.sigmoid(gate) * on).astype(BF)


def _retention(proj, decay, qd, kd, cd, gn, *, batch, seq):
    c = RET_CHUNK
    nq = RET_HEADS
    proj3 = proj.reshape(batch, seq, proj.shape[-1])
    return pl.pallas_call(
        _retention_kernel,
        grid=(batch, RET_HEADS, seq // c),
        in_specs=[
            pl.BlockSpec((1, c, RET_DK), lambda b, h, i: (b, i, h)),
            pl.BlockSpec((1, c, RET_DK), lambda b, h, i: (b, i, nq + h)),
            pl.BlockSpec((1, c, RET_DV), lambda b, h, i: (b, i, nq + h)),
            pl.BlockSpec((1, c, RET_DV), lambda b, h, i: (b, i, 2 * nq + h)),
            pl.BlockSpec((1, c, c), lambda b, h, i: (h, 0, 0)),
            pl.BlockSpec((1, c, RET_DV), lambda b, h, i: (h, 0, 0)),
            pl.BlockSpec((1, c, RET_DK), lambda b, h, i: (h, 0, 0)),
            pl.BlockSpec((1, 1, RET_DV), lambda b, h, i: (h, 0, 0)),
            pl.BlockSpec((1, RET_DV), lambda b, h, i: (0, h)),
        ],
        out_specs=pl.BlockSpec((1, c, RET_DV), lambda b, h, i: (b, i, h)),
        out_shape=jax.ShapeDtypeStruct((batch, seq, RET_HEADS * RET_DV), BF),
        scratch_shapes=[pltpu.VMEM((RET_DK, RET_DV), F32)],
        compiler_params=_cparams(3),
        name="retention",
    )(proj3, proj3, proj3, proj3, decay, qd, kd, cd, gn)


def _out_proj_kernel(y_ref, w_ref, g_ref, x_ref, o_ref):
    m = _dot(y_ref[...], w_ref[...])
    o_ref[...] = x_ref[...] + _rms_scale(m) * g_ref[...]


def _out_proj_res(y2d, w, g, x2d, *, tm=512):
    n, kdim = y2d.shape
    d = w.shape[1]
    return pl.pallas_call(
        _out_proj_kernel,
        grid=(n // tm,),
        in_specs=[
            pl.BlockSpec((tm, kdim), lambda i: (i, 0)),
            pl.BlockSpec((kdim, d), lambda i: (0, 0)),
            pl.BlockSpec((1, d), lambda i: (0, 0)),
            pl.BlockSpec((tm, d), lambda i: (i, 0)),
        ],
        out_specs=pl.BlockSpec((tm, d), lambda i: (i, 0)),
        out_shape=jax.ShapeDtypeStruct((n, d), F32),
        compiler_params=_cparams(1),
        name="out_proj_res",
    )(y2d, w, g, x2d)


def _ffn_kernel(x_ref, g_in_ref, wg_ref, wu_ref, wo_ref, g_out_ref, o_ref, hn_ref, acc_ref, *, n_ff):
    f = pl.program_id(1)

    @pl.when(f == 0)
    def _():
        hn_ref[...] = (_rms_scale(x_ref[...]) * g_in_ref[...]).astype(BF)
        acc_ref[...] = jnp.zeros_like(acc_ref)

    h = hn_ref[...]
    gate = _dot(h, wg_ref[...])
    up = _dot(h, wu_ref[...])
    act = (gate * jax.nn.sigmoid(gate) * up).astype(BF)
    acc_ref[...] += _dot(act, wo_ref[...])

    @pl.when(f == n_ff - 1)
    def _():
        o_ref[...] = x_ref[...] + _rms_scale(acc_ref[...]) * g_out_ref[...]


def _ffn(x2d, g_in, w_in, w_out, g_out, *, tm=512, tf=1408):
    n, d = x2d.shape
    n_ff = D_FF // tf
    kern = functools.partial(_ffn_kernel, n_ff=n_ff)
    return pl.pallas_call(
        kern,
        grid=(n // tm, n_ff),
        in_specs=[
            pl.BlockSpec((tm, d), lambda i, f: (i, 0)),
            pl.BlockSpec((1, d), lambda i, f: (0, 0)),
            pl.BlockSpec((d, tf), lambda i, f: (0, f)),
            pl.BlockSpec((d, tf), lambda i, f: (0, n_ff + f)),
            pl.BlockSpec((tf, d), lambda i, f: (f, 0)),
            pl.BlockSpec((1, d), lambda i, f: (0, 0)),
        ],
        out_specs=pl.BlockSpec((tm, d), lambda i, f: (i, 0)),
        out_shape=jax.ShapeDtypeStruct((n, d), F32),
        scratch_shapes=[pltpu.VMEM((tm, d), BF), pltpu.VMEM((tm, d), F32)],
        compiler_params=_cparams(2),
        name="ffn",
    )(x2d, g_in, w_in, w_in, w_out, g_out)


def _kv_proj_kernel(x_ref, g_ref, wn_ref, wt_ref, c_ref, s1_ref, s2_ref,
                    kc_ref, vc_ref, ks_ref, kw_ref, vst_ref, vwt_ref):
    hn = (_rms_scale(x_ref[...]) * g_ref[...]).astype(BF)
    tm = hn.shape[0]
    gw = NSA_GROUPS * NSA_HD
    rn = _dot(hn, wn_ref[...])
    cc, s1, s2 = c_ref[...], s1_ref[...], s2_ref[...]
    lane = lax.broadcasted_iota(jnp.int32, (tm, LANES), 1)
    row = lax.broadcasted_iota(jnp.int32, (tm, LANES), 0)
    blk = (row >> (L_SLC.bit_length() - 1)) & (SEL_BLOCKS - 1)
    indicator = jnp.where(lane == NSA_HD + blk, 1.0, 0.0)
    outs = (kc_ref, vc_ref, ks_ref, kw_ref)
    rotate = (True, False, True, True)
    for sec in range(4):
        for pair in range(gw // LANES):
            col = sec * gw + pair * LANES
            x = rn[:, col:col + LANES]
            if rotate[sec]:
                x = (x * cc + pltpu.roll(x, ROPE_HALF, axis=1) * s1
                     + pltpu.roll(x, LANES - ROPE_HALF, axis=1) * s2)
            if sec == 2:
                outs[sec][0, 2 * pair] = jnp.where(lane < NSA_HD, x, indicator).astype(BF)
                outs[sec][0, 2 * pair + 1] = jnp.where(lane < NSA_HD, pltpu.roll(x, NSA_HD, axis=1), indicator).astype(BF)
            else:
                xb = x.astype(BF)
                outs[sec][0, 2 * pair] = xb[:, :NSA_HD]
                outs[sec][0, 2 * pair + 1] = xb[:, NSA_HD:]
    rt = _dot_nt(wt_ref[...], hn).astype(BF)
    ones = jnp.ones((SEL_VROWS - NSA_HD, SEL_KEYS), BF)
    for g in range(NSA_GROUPS):
        for cb in range(tm // SEL_KEYS):
            vst_ref[0, g, cb, 0:NSA_HD, :] = rt[g * NSA_HD:(g + 1) * NSA_HD, cb * SEL_KEYS:(cb + 1) * SEL_KEYS]
            vst_ref[0, g, cb, NSA_HD:SEL_VROWS, :] = ones
        for cb in range(tm // LANES):
            vwt_ref[0, g, cb] = rt[gw + g * NSA_HD:gw + (g + 1) * NSA_HD, cb * LANES:(cb + 1) * LANES]


def _kv_proj(x2d, g, wn, wt, ctab, s1tab, s2tab, *, batch, seq, tm=512):
    n, d = x2d.shape
    tpos = seq // tm
    gdim, hd = NSA_GROUPS, NSA_HD
    normal = jax.ShapeDtypeStruct((batch, gdim, seq, hd), BF)
    normal_spec = pl.BlockSpec((1, gdim, tm, hd), lambda i: (i // tpos, 0, i % tpos, 0))
    tab_spec = pl.BlockSpec((tm, LANES), lambda i: (i % tpos, 0))
    return pl.pallas_call(
        _kv_proj_kernel,
        grid=(n // tm,),
        in_specs=[
            pl.BlockSpec((tm, d), lambda i: (i, 0)),
            pl.BlockSpec((1, d), lambda i: (0, 0)),
            pl.BlockSpec(wn.shape, lambda i: (0, 0)),
            pl.BlockSpec(wt.shape, lambda i: (0, 0)),
            tab_spec, tab_spec, tab_spec,
        ],
        out_specs=[
            normal_spec, normal_spec,
            pl.BlockSpec((1, gdim, tm, LANES), lambda i: (i // tpos, 0, i % tpos, 0)),
            normal_spec,
            pl.BlockSpec((1, gdim, tm // SEL_KEYS, SEL_VROWS, SEL_KEYS), lambda i: (i // tpos, 0, i % tpos, 0, 0)),
            pl.BlockSpec((1, gdim, tm // LANES, hd, LANES), lambda i: (i // tpos, 0, i % tpos, 0, 0)),
        ],
        out_shape=[
            normal, normal, jax.ShapeDtypeStruct((batch, gdim, seq, LANES), BF), normal,
            jax.ShapeDtypeStruct((batch, gdim, seq // SEL_KEYS, SEL_VROWS, SEL_KEYS), BF),
            jax.ShapeDtypeStruct((batch, gdim, seq // LANES, hd, LANES), BF),
        ],
        compiler_params=_cparams(1),
        name="kv_proj",
    )(x2d, g, wn, wt, ctab, s1tab, s2tab)


def _gelu_tanh(x):
    return 0.5 * x * (1.0 + jnp.tanh(0.7978845608028654 * (x + 0.044715 * (x * x * x))))


def _compress_kernel(x_ref, pos_ref, w1_ref, w1cat_ref, w2_ref, o_ref, *, transposed_out):
    half = CMP_STRIDE * NSA_HD
    pb = _dot(pos_ref[...], w1_ref[...])[0:1]
    first, second = [], []
    for k in range(4):
        res = _dot(x_ref[0, 0, :, k * half:(k + 1) * half], w1cat_ref[...])
        first.append(res[:, :CMP_HIDDEN])
        second.append(res[:, CMP_HIDDEN:])
    rows = first[0].shape[0]
    for k in range(4):
        nxt = second[k + 1] if k < 3 else pltpu.roll(second[0], rows - 1, axis=0)
        hid = _gelu_tanh(first[k] + nxt + pb).astype(BF)
        if transposed_out:
            o_ref[0, 0, :, k * rows:(k + 1) * rows] = _dot_nt(w2_ref[...], hid).astype(BF)
        else:
            o_ref[0, 0, k * rows:(k + 1) * rows, :] = _dot(hid, w2_ref[...]).astype(BF)


def _compress(x4, pos8, w1, w1cat, w2, *, transposed_out):
    batch, gdim, rows, feat = x4.shape
    n_rows = 4 * rows
    if transposed_out:
        out_shape = jax.ShapeDtypeStruct((batch, gdim, NSA_HD, n_rows), BF)
        out_spec = pl.BlockSpec((1, 1, NSA_HD, n_rows), lambda b, g: (b, g, 0, 0))
    else:
        out_shape = jax.ShapeDtypeStruct((batch, gdim, n_rows, NSA_HD), BF)
        out_spec = pl.BlockSpec((1, 1, n_rows, NSA_HD), lambda b, g: (b, g, 0, 0))
    kern = functools.partial(_compress_kernel, transposed_out=transposed_out)
    return pl.pallas_call(
        kern,
        grid=(batch, gdim),
        in_specs=[
            pl.BlockSpec((1, 1, rows, feat), lambda b, g: (b, g, 0, 0)),
            pl.BlockSpec(pos8.shape, lambda b, g: (0, 0)),
            pl.BlockSpec(w1.shape, lambda b, g: (0, 0)),
            pl.BlockSpec(w1cat.shape, lambda b, g: (0, 0)),
            pl.BlockSpec(w2.shape, lambda b, g: (0, 0)),
        ],
        out_specs=out_spec,
        out_shape=out_shape,
        compiler_params=_cparams(2),
        name="compress_v" if transposed_out else "compress_k",
    )(x4, pos8, w1, w1cat, w2)


def _nsa_q_proj_kernel(x_ref, g_ref, wq_ref, wg_ref, cos_ref, sin_ref, q_ref, gate_ref):
    hn = (_rms_scale(x_ref[...]) * g_ref[...]).astype(BF)
    tm = hn.shape[0]
    qt = _dot_nt(wq_ref[...], hn) * (NSA_HD ** -0.5 * LOG2E)
    cos, sin = cos_ref[...], sin_ref[...]
    for h in range(NSA_HEADS):
        g, r = divmod(h, NSA_REP)
        base = h * NSA_HD
        x1 = qt[base:base + ROPE_HALF]
        x2 = qt[base + ROPE_HALF:base + ROPE_DIMS]
        head = jnp.concatenate(
            [x1 * cos - x2 * sin, x1 * sin + x2 * cos, qt[base + ROPE_DIMS:base + NSA_HD]], axis=0).astype(BF)
        for blk in range(tm // Q_BLOCK):
            col = blk * NSA_REP * Q_BLOCK + r * Q_BLOCK
            q_ref[0, g, :, col:col + Q_BLOCK] = head[:, blk * Q_BLOCK:(blk + 1) * Q_BLOCK]
    gt = jax.nn.sigmoid(_dot_nt(wg_ref[...], hn))
    for g in range(NSA_GROUPS):
        gate_ref[0, g] = gt[g * 16:(g + 1) * 16]


def _nsa_q_proj(x2d, g, wqt, wgt, cos_t, sin_t, *, batch, seq, tm=512):
    n, d = x2d.shape
    tpos = seq // tm
    qcols = NSA_REP * Q_BLOCK
    return pl.pallas_call(
        _nsa_q_proj_kernel,
        grid=(n // tm,),
        in_specs=[
            pl.BlockSpec((tm, d), lambda i: (i, 0)),
            pl.BlockSpec((1, d), lambda i: (0, 0)),
            pl.BlockSpec(wqt.shape, lambda i: (0, 0)),
            pl.BlockSpec(wgt.shape, lambda i: (0, 0)),
            pl.BlockSpec((ROPE_HALF, tm), lambda i: (0, i % tpos)),
            pl.BlockSpec((ROPE_HALF, tm), lambda i: (0, i % tpos)),
        ],
        out_specs=[
            pl.BlockSpec((1, NSA_GROUPS, NSA_HD, (tm // Q_BLOCK) * qcols), lambda i: (i // tpos, 0, 0, i % tpos)),
            pl.BlockSpec((1, NSA_GROUPS, 16, tm), lambda i: (i // tpos, 0, 0, i % tpos)),
        ],
        out_shape=[
            jax.ShapeDtypeStruct((batch, NSA_GROUPS, NSA_HD, (seq // Q_BLOCK) * qcols), BF),
            jax.ShapeDtypeStruct((batch, NSA_GROUPS, 16, seq), F32),
        ],
        compiler_params=_cparams(1),
        name="nsa_q_proj",
    )(x2d, g, wqt, wgt, cos_t, sin_t)


def _softmax_cols(s):
    m = jnp.max(s, axis=0, keepdims=True)
    p = jnp.exp2(s - m)
    return p, jnp.sum(p, axis=0, keepdims=True)


def _nsa_attn_kernel(q_ref, gate_ref, ck_ref, cvt_ref, ks_ref, vst_ref, kw_ref, vwt_ref, o_ref,
                     mt_ref, m_ref, acc_ref, qa_ref, sc_ref):
    gp = q_ref.shape[1]
    bi = pl.program_id(2)
    qcols = NSA_REP * Q_BLOCK
    lane = lax.broadcasted_iota(jnp.int32, (1, qcols), 1)
    t = bi * Q_BLOCK + (lane & (Q_BLOCK - 1))

    n_rows = ck_ref.shape[2]
    n_slc = n_rows // 4
    rho = lax.broadcasted_iota(jnp.int32, (n_rows, 1), 0)
    slc_bits = n_slc.bit_length() - 1
    n_idx = 4 * (rho & (n_slc - 1)) + (rho >> slc_bits)
    valid = (n_idx * CMP_STRIDE + (L_CMP - 1)) <= t
    j_io = lax.broadcasted_iota(jnp.int32, (n_slc, Q_BLOCK), 0)
    oc, imps = [], []
    for g in range(gp):
        s = _dot(ck_ref[0, g], q_ref[0, g])
        p, l = _softmax_cols(jnp.where(valid, s, NEG))
        pc = p * jnp.where(t >= L_CMP - 1, 1.0 / l, 0.0)
        oc.append(_dot(cvt_ref[0, g], pc.astype(BF)))
        ps = pc[:, 0:Q_BLOCK]
        for r in range(1, NSA_REP):
            ps = ps + pc[:, r * Q_BLOCK:(r + 1) * Q_BLOCK]
        p0, p1, p2, p3 = (ps[k * n_slc:(k + 1) * n_slc] for k in range(4))
        p3_prev = jnp.where(j_io == 0, 0.0, pltpu.roll(p3, 1, axis=0))
        imps.append(p3_prev + 2.0 * p0 + 2.0 * p1 + 2.0 * p2 + p3)
    imp = jnp.concatenate(imps, axis=1) if gp > 1 else imps[0]

    j_w = lax.broadcasted_iota(jnp.int32, imp.shape, 0)
    tq = bi * Q_BLOCK + (lax.broadcasted_iota(jnp.int32, (1, imp.shape[1]), 1) & (Q_BLOCK - 1))
    cur = tq >> (L_SLC.bit_length() - 1)
    forced = (j_w == 0) | (j_w == cur) | (j_w == cur - 1)
    val = jnp.where(forced, BIG, jnp.where(j_w <= cur, imp, NEG))
    j_f = j_w.astype(F32)
    mt = jnp.full(imp.shape, NEG, F32)
    for _ in range(min(N_SELECT, n_slc)):
        mx = jnp.max(val, axis=0, keepdims=True)
        idx = jnp.min(jnp.where(val == mx, j_f, float(n_slc)), axis=0, keepdims=True)
        hit = j_f == idx
        mt = jnp.where(hit, 0.0, mt)
        val = jnp.where(hit, TAKEN, val)
    for g in range(gp):
        for r in range(NSA_REP):
            mt_ref[g, :, r * Q_BLOCK:(r + 1) * Q_BLOCK] = mt[:, g * Q_BLOCK:(g + 1) * Q_BLOCK]

    for g in range(gp):
        m_ref[g] = jnp.full((1, qcols), NEG, F32)
        acc_ref[g] = jnp.zeros((SEL_VROWS, qcols), F32)
        qa_ref[g, 0:NSA_HD, :] = q_ref[0, g]
        qa_ref[g, NSA_HD:, :] = jnp.zeros((LANES - NSA_HD, qcols), BF)
    row_io = lax.broadcasted_iota(jnp.int32, (SEL_KEYS, 1), 0)
    pad_rows = jnp.zeros((16 - SEL_BLOCKS, qcols), BF)

    def sel_scores(c, slot):
        for g in range(gp):
            bias = mt_ref[g, pl.ds(pl.multiple_of(c * SEL_BLOCKS, SEL_BLOCKS), SEL_BLOCKS), :].astype(BF)
            qa_ref[g, NSA_HD:NSA_HD + 16, :] = jnp.concatenate([bias, pad_rows], axis=0)
            k = ks_ref[0, g, pl.ds(pl.multiple_of(c * SEL_KEYS, SEL_KEYS), SEL_KEYS), :]
            sc_ref[slot, g] = _dot(k, qa_ref[g])

    def sel_accumulate(c, slot, causal):
        for g in range(gp):
            sc = sc_ref[slot, g]
            if causal:
                sc = jnp.where((c * SEL_KEYS + row_io) <= t, sc, NEG)
            m_old = m_ref[g]
            m_new = jnp.maximum(m_old, jnp.max(sc, axis=0, keepdims=True))
            alpha = jnp.exp2(m_old - m_new)
            pr = jnp.exp2(sc - m_new).astype(BF)
            acc_ref[g] = alpha * acc_ref[g] + _dot(vst_ref[0, g, c], pr)
            m_ref[g] = m_new

    def sel_pair(k, carry):
        c = 2 * k
        sel_scores(c + 1, 1)
        sel_accumulate(c, 0, False)
        sel_scores(c + 2, 0)
        sel_accumulate(c + 1, 1, False)
        return carry

    n_full = (bi * Q_BLOCK) >> (SEL_KEYS.bit_length() - 1)
    sel_scores(0, 0)
    lax.fori_loop(0, n_full >> 1, sel_pair, 0)

    @pl.when((n_full & 1) == 0)
    def _():
        sel_accumulate(n_full, 0, True)

    @pl.when((n_full & 1) == 1)
    def _():
        sel_scores(n_full, 1)
        sel_accumulate(n_full - 1, 0, False)
        sel_accumulate(n_full, 1, True)

    w_keys = WINDOW + Q_BLOCK
    wb = jnp.maximum(bi - WINDOW // Q_BLOCK, 0)
    start = pl.multiple_of(wb * Q_BLOCK, Q_BLOCK)
    dist = t - (start + lax.broadcasted_iota(jnp.int32, (w_keys, 1), 0))
    in_window = (dist >= 0) & (dist < WINDOW)
    out_w = NSA_REP * NSA_HD
    for g in range(gp):
        kw = kw_ref[0, g, pl.ds(start, w_keys), :]
        sw = _dot(kw, q_ref[0, g])
        pw, lw = _softmax_cols(jnp.where(in_window, sw, NEG))
        vw = jnp.concatenate([vwt_ref[0, g, wb + w] for w in range(w_keys // LANES)], axis=1)
        ow = _dot(vw, pw.astype(BF)) * (1.0 / lw)
        osel = acc_ref[g, 0:NSA_HD, :] * (1.0 / acc_ref[g, NSA_HD:NSA_HD + 1, :])

        heads = []
        for r in range(NSA_REP):
            sl = slice(r * Q_BLOCK, (r + 1) * Q_BLOCK)
            gc = gate_ref[0, g, 0 * NSA_REP + r:0 * NSA_REP + r + 1, :]
            gs = gate_ref[0, g, 1 * NSA_REP + r:1 * NSA_REP + r + 1, :]
            gw = gate_ref[0, g, 2 * NSA_REP + r:2 * NSA_REP + r + 1, :]
            heads.append(gc * oc[g][:, sl] + gs * osel[:, sl] + gw * ow[:, sl])
        o_ref[0, :, g * out_w:(g + 1) * out_w] = jnp.concatenate(heads, axis=0).T.astype(BF)


def _nsa_attn(qt, gates, ck, cvt, ks, vst, kw, vwt, *, batch, seq, gp=2):
    nb = seq // Q_BLOCK
    qcols = NSA_REP * Q_BLOCK
    n_cmp_rows = ck.shape[2]
    hd = NSA_HD
    full = lambda shape: pl.BlockSpec((1, gp) + tuple(shape[2:]), lambda b, g, i: (b, g) + (0,) * (len(shape) - 2))
    return pl.pallas_call(
        _nsa_attn_kernel,
        grid=(batch, NSA_GROUPS // gp, nb),
        in_specs=[
            pl.BlockSpec((1, gp, hd, qcols), lambda b, g, i: (b, g, 0, i)),
            pl.BlockSpec((1, gp, 16, Q_BLOCK), lambda b, g, i: (b, g, 0, i)),
            full(ck.shape), full(cvt.shape), full(ks.shape), full(vst.shape), full(kw.shape), full(vwt.shape),
        ],
        out_specs=pl.BlockSpec((1, Q_BLOCK, gp * NSA_REP * hd), lambda b, g, i: (b, i, g)),
        out_shape=jax.ShapeDtypeStruct((batch, seq, NSA_HEADS * hd), BF),
        scratch_shapes=[
            pltpu.VMEM((gp, n_cmp_rows // 4, qcols), F32),
            pltpu.VMEM((gp, 1, qcols), F32),
            pltpu.VMEM((gp, SEL_VROWS, qcols), F32),
            pltpu.VMEM((gp, LANES, qcols), BF),
            pltpu.VMEM((2, gp, SEL_KEYS, qcols), F32),
        ],
        compiler_params=_cparams(3),
        name="nsa_attn",
    )(qt, gates, ck, cvt, ks, vst, kw, vwt)


def _retention_tables(seq):
    c = RET_CHUNK
    freqs = 1.0 / (RET_ROT_BASE ** jnp.linspace(0.0, 1.0, RET_DK // 2, dtype=F32))
    ang = jnp.arange(seq, dtype=F32)[:, None] * freqs[None, :]
    log_g = jnp.log1p(-jnp.exp2(-5.0 - jnp.arange(RET_HEADS, dtype=F32)))
    idx = jnp.arange(c, dtype=F32)
    diff = idx[:, None] - idx[None, :]
    decay = jnp.where(diff >= 0, jnp.exp(log_g[:, None, None] * jnp.maximum(diff, 0.0)), 0.0)
    qd = jnp.exp(log_g[:, None] * (idx + 1.0))[:, :, None]
    kd = jnp.exp(log_g[:, None] * (c - 1.0 - idx))[:, :, None]
    cd = jnp.exp(log_g * c)[:, None, None]
    return (jnp.cos(ang), jnp.sin(ang), decay,
            jnp.broadcast_to(qd, (RET_HEADS, c, RET_DV)),
            jnp.broadcast_to(kd, (RET_HEADS, c, RET_DK)),
            jnp.broadcast_to(cd, (RET_HEADS, 1, RET_DV)))


def _rope_tables(seq):
    freqs = ROPE_THETA ** (-jnp.arange(0, ROPE_DIMS, 2, dtype=F32) / ROPE_DIMS)
    ang = jnp.arange(seq, dtype=F32)[:, None] * freqs[None, :]
    cos, sin = jnp.cos(ang), jnp.sin(ang)
    ones = jnp.ones((seq, NSA_HD - ROPE_DIMS), F32)
    zeros8 = jnp.zeros((seq, ROPE_HALF), F32)
    zeros_rest = jnp.zeros((seq, NSA_HD - ROPE_DIMS), F32)
    ctab = jnp.concatenate([cos, cos, ones], axis=1)
    s1 = jnp.concatenate([zeros8, sin, zeros_rest], axis=1)
    s2 = jnp.concatenate([-sin, zeros8, zeros_rest], axis=1)
    two = lambda a: jnp.concatenate([a, a], axis=1)
    return two(ctab), two(s1), two(s2), cos.T, sin.T


def kernel(x, norms, ret_w_in, ret_gn, ret_w_out, nsa_w_in, nsa_w_out, kv_norm, w_kv, cmp_pos_k, cmp_pos_v,
           cmp_w1_k, cmp_w2_k, cmp_w1_v, cmp_w2_v, ffn_w_in, ffn_w_out):
    batch, seq, d = x.shape
    n = batch * seq
    x2d = x.reshape(n, d)
    norm = lambda layer, k: norms[layer, k][None, :]

    cos_r, sin_r, decay, qd, kd, cd = _retention_tables(seq)
    proj = _ret_proj(x2d, norm(0, 0), ret_w_in[0].astype(BF), cos_r, sin_r, seq=seq)
    y = _retention(proj, decay, qd, kd, cd, ret_gn[0][None, :], batch=batch, seq=seq)
    x2d = _out_proj_res(y.reshape(n, -1), ret_w_out[0].astype(BF), norm(0, 1), x2d)
    x2d = _ffn(x2d, norm(0, 2), ffn_w_in[0].astype(BF), ffn_w_out[0].astype(BF), norm(0, 3))

    gw = NSA_GROUPS * NSA_HD
    sec = lambda s: w_kv[:, s * gw:(s + 1) * gw]
    wn = jnp.concatenate([sec(0), sec(1), sec(2), sec(4)], axis=1).astype(BF)
    wt = jnp.concatenate([sec(3), sec(5)], axis=1).T.astype(BF)
    ctab, s1tab, s2tab, cos_t, sin_t = _rope_tables(seq)
    k_cmp, v_cmp, k_slc, k_win, v_slc_t, v_win_t = _kv_proj(
        x2d, kv_norm[None, :], wn, wt, ctab, s1tab, s2tab, batch=batch, seq=seq)

    half = CMP_STRIDE * NSA_HD
    rows = seq // (4 * CMP_STRIDE)

    def compress(a, pos, w1, w2, transposed_out):
        x4 = a.reshape(batch, NSA_GROUPS, rows, 4 * half)
        pos8 = jnp.broadcast_to(pos.reshape(1, L_CMP * NSA_HD), (8, L_CMP * NSA_HD)).astype(BF)
        w1cat = jnp.concatenate([w1[:half], w1[half:]], axis=1).astype(BF)
        w2p = (w2.T if transposed_out else w2).astype(BF)
        return _compress(x4, pos8, w1.astype(BF), w1cat, w2p, transposed_out=transposed_out)

    ck = compress(k_cmp, cmp_pos_k, cmp_w1_k, cmp_w2_k, False)
    cvt = compress(v_cmp, cmp_pos_v, cmp_w1_v, cmp_w2_v, True)

    hq = NSA_HEADS * NSA_HD
    wqt = nsa_w_in[0][:, :hq].T.astype(BF)
    wg = nsa_w_in[0][:, hq:].reshape(d, NSA_GROUPS, NSA_REP, 3).transpose(0, 1, 3, 2)
    wg = jnp.pad(wg.reshape(d, NSA_GROUPS, 3 * NSA_REP), ((0, 0), (0, 0), (0, 16 - 3 * NSA_REP)))
    wgt = wg.reshape(d, NSA_GROUPS * 16).T.astype(BF)
    qt, gates = _nsa_q_proj(x2d, norm(1, 0), wqt, wgt, cos_t, sin_t, batch=batch, seq=seq)
    o = _nsa_attn(qt, gates, ck, cvt, k_slc, v_slc_t, k_win, v_win_t, batch=batch, seq=seq)
    x2d = _out_proj_res(o.reshape(n, -1), nsa_w_out[0].astype(BF), norm(1, 1), x2d)
    x2d = _ffn(x2d, norm(1, 2), ffn_w_in[1].astype(BF), ffn_w_out[1].astype(BF), norm(1, 3))
    return x2d.reshape(batch, seq, d)
```

```python
import functools

import jax
import jax.numpy as jnp
import numpy as np
from jax import lax
from jax.experimental import pallas as pl
from jax.experimental.pallas import tpu as pltpu

BF = jnp.bfloat16
F32 = jnp.float32

D_MODEL = 1024
RET_HEADS = 4
RET_DK = 256
RET_DV = 512
RET_CHUNK = 128
RET_ROT_BASE = 10000.0
NSA_HEADS = 16
NSA_GROUPS = 4
NSA_HD = 64
NSA_REP = 4
L_CMP = 32
CMP_STRIDE = 16
L_SLC = 64
N_SELECT = 16
WINDOW = 512
CMP_HIDDEN = 256
Q_BLOCK = 128
ROPE_THETA = 500000.0
ROPE_DIMS = 16
ROPE_HALF = ROPE_DIMS // 2
D_FF = 2816
EPS = 1e-6
NEG = -1e30
TAKEN = -3e38

LANES = 128
SEL_KEYS = 512
SEL_BLOCKS = SEL_KEYS // L_SLC
SEL_VROWS = NSA_HD + 16
LOG2E = 1.4426950408889634
VMEM_LIMIT = 48 * 1024 * 1024


def _cparams(n_axes):
    return pltpu.CompilerParams(dimension_semantics=("arbitrary",) * n_axes,
                                vmem_limit_bytes=VMEM_LIMIT)


def _rms_scale(x):
    return x * lax.rsqrt(jnp.mean(x * x, axis=-1, keepdims=True) + EPS)


def _dot(a, b):
    return jnp.dot(a, b, preferred_element_type=F32)


def _dot_nt(a, b):
    return lax.dot_general(a, b, (((1,), (1,)), ((), ())), preferred_element_type=F32)


def _dot_tn(a, b):
    return lax.dot_general(a, b, (((0,), (0,)), ((), ())), preferred_element_type=F32)


def _ret_proj_kernel(x_ref, g_ref, w_ref, cos_ref, sin_ref, o_ref, hn_ref):
    j = pl.program_id(1)
    half = RET_DK // 2

    @pl.when(j == 0)
    def _():
        hn_ref[...] = (_rms_scale(x_ref[...]) * g_ref[...]).astype(BF)
        cos = cos_ref[...]
        sin = sin_ref[...]
        for h in range(2 * RET_HEADS):
            acc = _dot(hn_ref[...], w_ref[0, :, h * RET_DK:(h + 1) * RET_DK])
            scale = RET_DK ** -0.5 if h >= RET_HEADS else 1.0
            x1 = acc[:, :half]
            x2 = acc[:, half:]
            o_ref[:, h * RET_DK:h * RET_DK + half] = ((x1 * cos - x2 * sin) * scale).astype(BF)
            o_ref[:, h * RET_DK + half:(h + 1) * RET_DK] = ((x1 * sin + x2 * cos) * scale).astype(BF)

    @pl.when(j > 0)
    def _():
        step = 2 * RET_DK
        for c in range(o_ref.shape[1] // step):
            o_ref[:, c * step:(c + 1) * step] = _dot(hn_ref[...], w_ref[0, :, c * step:(c + 1) * step]).astype(BF)


def _ret_proj(x2d, g, w3, cos, sin, *, seq, tm=512):
    n, d = x2d.shape
    n_tiles, _, tn = w3.shape
    tpos = seq // tm
    return pl.pallas_call(
        _ret_proj_kernel,
        grid=(n // tm, n_tiles),
        in_specs=[
            pl.BlockSpec((tm, d), lambda i, j: (i, 0)),
            pl.BlockSpec((1, d), lambda i, j: (0, 0)),
            pl.BlockSpec((1, d, tn), lambda i, j: (j, 0, 0)),
            pl.BlockSpec((tm, RET_DK // 2), lambda i, j: (i % tpos, 0)),
            pl.BlockSpec((tm, RET_DK // 2), lambda i, j: (i % tpos, 0)),
        ],
        out_specs=pl.BlockSpec((tm, tn), lambda i, j: (i, j)),
        out_shape=jax.ShapeDtypeStruct((n, n_tiles * tn), BF),
        scratch_shapes=[pltpu.VMEM((tm, d), BF)],
        compiler_params=_cparams(2),
        name="ret_proj",
    )(x2d, g, w3, cos, sin)


def _retention_kernel(q_ref, k_ref, v_ref, g_ref, decay_ref, qd_ref, kd_ref, cd_ref, gn_ref, o_ref, state_ref):
    c = pl.program_id(1)

    @pl.when(c == 0)
    def _():
        state_ref[...] = jnp.zeros_like(state_ref)

    for h in range(RET_HEADS):
        ks = slice(h * RET_DK, (h + 1) * RET_DK)
        vs = slice(h * RET_DV, (h + 1) * RET_DV)
        q = q_ref[0, :, ks]
        k = k_ref[0, :, ks]
        v = v_ref[0, :, vs]
        scores = _dot_nt(q, k) * decay_ref[h]
        inner = _dot(scores.astype(BF), v)
        state = state_ref[h]
        cross = _dot(q, state.astype(BF)) * qd_ref[h]
        kk = (k.astype(F32) * kd_ref[h]).astype(BF)
        state_ref[h] = state * cd_ref[h] + _dot_tn(kk, v)
        o = inner + cross
        mu = jnp.mean(o, axis=-1, keepdims=True)
        dlt = o - mu
        var = jnp.mean(dlt * dlt, axis=-1, keepdims=True)
        on = dlt * lax.rsqrt(var + EPS) * gn_ref[:, vs]
        gate = g_ref[0, :, vs].astype(F32)
        o_ref[0, :, vs] = (gate * jax.nn.sigmoid(gate) * on).astype(BF)


def _retention(proj, decay, qd, kd, cd, gn, *, batch, seq):
    c = RET_CHUNK
    qk_w = RET_HEADS * RET_DK
    v_w = RET_HEADS * RET_DV
    proj3 = proj.reshape(batch, seq, proj.shape[-1])
    const = lambda a: pl.BlockSpec(a.shape, lambda b, i: (0,) * a.ndim)
    return pl.pallas_call(
        _retention_kernel,
        grid=(batch, seq // c),
        in_specs=[
            pl.BlockSpec((1, c, qk_w), lambda b, i: (b, i, 0)),
            pl.BlockSpec((1, c, qk_w), lambda b, i: (b, i, 1)),
            pl.BlockSpec((1, c, v_w), lambda b, i: (b, i, 1)),
            pl.BlockSpec((1, c, v_w), lambda b, i: (b, i, 2)),
            const(decay), const(qd), const(kd), const(cd), const(gn),
        ],
        out_specs=pl.BlockSpec((1, c, v_w), lambda b, i: (b, i, 0)),
        out_shape=jax.ShapeDtypeStruct((batch, seq, v_w), BF),
        scratch_shapes=[pltpu.VMEM((RET_HEADS, RET_DK, RET_DV), F32)],
        compiler_params=_cparams(2),
        name="retention",
    )(proj3, proj3, proj3, proj3, decay, qd, kd, cd, gn)


FFN_CHUNKS = ((0, 1024), (1024, 2048), (2048, D_FF))


def _mix_ffn_kernel(y_ref, x_ref, wm_ref, gm_ref, g_in_ref, wi_ref, wo_ref, g_out_ref, o_ref):
    x1 = x_ref[...] + _rms_scale(_dot(y_ref[...], wm_ref[...])) * gm_ref[...]
    h = (_rms_scale(x1) * g_in_ref[...]).astype(BF)
    acc = None
    for lo, hi in FFN_CHUNKS:
        gate = _dot(h, wi_ref[:, lo:hi])
        up = _dot(h, wi_ref[:, D_FF + lo:D_FF + hi])
        act = (gate * jax.nn.sigmoid(gate) * up).astype(BF)
        part = _dot(act, wo_ref[lo:hi, :])
        acc = part if acc is None else acc + part
    o_ref[...] = x1 + _rms_scale(acc) * g_out_ref[...]


def _mix_ffn(y2d, x2d, w_mix, g_mix, g_in, w_in, w_out, g_out, *, tm=512):
    n, d = x2d.shape
    kdim = y2d.shape[1]
    resident = lambda a: pl.BlockSpec(a.shape, lambda i: (0,) * a.ndim, pipeline_mode=pl.Buffered(1))
    row = lambda w: pl.BlockSpec((tm, w), lambda i: (i, 0))
    return pl.pallas_call(
        _mix_ffn_kernel,
        grid=(n // tm,),
        in_specs=[row(kdim), row(d), resident(w_mix), resident(g_mix), resident(g_in),
                  resident(w_in), resident(w_out), resident(g_out)],
        out_specs=row(d),
        out_shape=jax.ShapeDtypeStruct((n, d), F32),
        compiler_params=_cparams(1),
        name="mix_ffn",
    )(y2d, x2d, w_mix, g_mix, g_in, w_in, w_out, g_out)


def _kv_proj_kernel(x_ref, g_ref, wn_ref, wt_ref, c_ref, s1_ref, s2_ref,
                    kc_ref, vc_ref, ks_ref, kw_ref, vst_ref, vwt_ref):
    hn = (_rms_scale(x_ref[...]) * g_ref[...]).astype(BF)
    tm = hn.shape[0]
    gw = NSA_GROUPS * NSA_HD
    rn = _dot(hn, wn_ref[...])
    cc, s1, s2 = c_ref[...], s1_ref[...], s2_ref[...]
    lane = lax.broadcasted_iota(jnp.int32, (tm, LANES), 1)
    row = lax.broadcasted_iota(jnp.int32, (tm, LANES), 0)
    blk = (row >> (L_SLC.bit_length() - 1)) & (SEL_BLOCKS - 1)
    indicator = jnp.where(lane == NSA_HD + blk, 1.0, 0.0)
    outs = (kc_ref, vc_ref, ks_ref, kw_ref)
    rotate = (True, False, True, True)
    for sec in range(4):
        for pair in range(gw // LANES):
            col = sec * gw + pair * LANES
            x = rn[:, col:col + LANES]
            if rotate[sec]:
                x = (x * cc + pltpu.roll(x, ROPE_HALF, axis=1) * s1
                     + pltpu.roll(x, LANES - ROPE_HALF, axis=1) * s2)
            if sec == 2:
                outs[sec][0, 2 * pair] = jnp.where(lane < NSA_HD, x, indicator).astype(BF)
                outs[sec][0, 2 * pair + 1] = jnp.where(lane < NSA_HD, pltpu.roll(x, NSA_HD, axis=1), indicator).astype(BF)
            else:
                xb = x.astype(BF)
                outs[sec][0, 2 * pair] = xb[:, :NSA_HD]
                outs[sec][0, 2 * pair + 1] = xb[:, NSA_HD:]
    rt = _dot_nt(wt_ref[...], hn).astype(BF)
    ones = jnp.ones((SEL_VROWS - NSA_HD, SEL_KEYS), BF)
    for g in range(NSA_GROUPS):
        for cb in range(tm // SEL_KEYS):
            vst_ref[0, g, cb, 0:NSA_HD, :] = rt[g * NSA_HD:(g + 1) * NSA_HD, cb * SEL_KEYS:(cb + 1) * SEL_KEYS]
            vst_ref[0, g, cb, NSA_HD:SEL_VROWS, :] = ones
        for cb in range(tm // LANES):
            vwt_ref[0, g, cb] = rt[gw + g * NSA_HD:gw + (g + 1) * NSA_HD, cb * LANES:(cb + 1) * LANES]


def _kv_proj(x2d, g, wn, wt, ctab, s1tab, s2tab, *, batch, seq, tm=512):
    n, d = x2d.shape
    tpos = seq // tm
    gdim, hd = NSA_GROUPS, NSA_HD
    normal = jax.ShapeDtypeStruct((batch, gdim, seq, hd), BF)
    normal_spec = pl.BlockSpec((1, gdim, tm, hd), lambda i: (i // tpos, 0, i % tpos, 0))
    tab_spec = pl.BlockSpec((tm, LANES), lambda i: (i % tpos, 0))
    return pl.pallas_call(
        _kv_proj_kernel,
        grid=(n // tm,),
        in_specs=[
            pl.BlockSpec((tm, d), lambda i: (i, 0)),
            pl.BlockSpec((1, d), lambda i: (0, 0)),
            pl.BlockSpec(wn.shape, lambda i: (0, 0)),
            pl.BlockSpec(wt.shape, lambda i: (0, 0)),
            tab_spec, tab_spec, tab_spec,
        ],
        out_specs=[
            normal_spec, normal_spec,
            pl.BlockSpec((1, gdim, tm, LANES), lambda i: (i // tpos, 0, i % tpos, 0)),
            normal_spec,
            pl.BlockSpec((1, gdim, tm // SEL_KEYS, SEL_VROWS, SEL_KEYS), lambda i: (i // tpos, 0, i % tpos, 0, 0)),
            pl.BlockSpec((1, gdim, tm // LANES, hd, LANES), lambda i: (i // tpos, 0, i % tpos, 0, 0)),
        ],
        out_shape=[
            normal, normal, jax.ShapeDtypeStruct((batch, gdim, seq, LANES), BF), normal,
            jax.ShapeDtypeStruct((batch, gdim, seq // SEL_KEYS, SEL_VROWS, SEL_KEYS), BF),
            jax.ShapeDtypeStruct((batch, gdim, seq // LANES, hd, LANES), BF),
        ],
        compiler_params=_cparams(1),
        name="kv_proj",
    )(x2d, g, wn, wt, ctab, s1tab, s2tab)


def _gelu_tanh(x):
    return 0.5 * x * (1.0 + jnp.tanh(0.7978845608028654 * (x + 0.044715 * (x * x * x))))


def _compress_kernel(x_ref, pos_ref, w1_ref, w1cat_ref, w2_ref, o_ref, *, transposed_out):
    half = CMP_STRIDE * NSA_HD
    pb = _dot(pos_ref[...], w1_ref[...])[0:1]
    first, second = [], []
    for k in range(4):
        res = _dot(x_ref[0, 0, :, k * half:(k + 1) * half], w1cat_ref[...])
        first.append(res[:, :CMP_HIDDEN])
        second.append(res[:, CMP_HIDDEN:])
    rows = first[0].shape[0]
    for k in range(4):
        nxt = second[k + 1] if k < 3 else pltpu.roll(second[0], rows - 1, axis=0)
        hid = _gelu_tanh(first[k] + nxt + pb).astype(BF)
        if transposed_out:
            o_ref[0, 0, :, k * rows:(k + 1) * rows] = _dot_nt(w2_ref[...], hid).astype(BF)
        else:
            o_ref[0, 0, k * rows:(k + 1) * rows, :] = _dot(hid, w2_ref[...]).astype(BF)


def _compress(x4, pos8, w1, w1cat, w2, *, transposed_out):
    batch, gdim, rows, feat = x4.shape
    n_rows = 4 * rows
    if transposed_out:
        out_shape = jax.ShapeDtypeStruct((batch, gdim, NSA_HD, n_rows), BF)
        out_spec = pl.BlockSpec((1, 1, NSA_HD, n_rows), lambda b, g: (b, g, 0, 0))
    else:
        out_shape = jax.ShapeDtypeStruct((batch, gdim, n_rows, NSA_HD), BF)
        out_spec = pl.BlockSpec((1, 1, n_rows, NSA_HD), lambda b, g: (b, g, 0, 0))
    kern = functools.partial(_compress_kernel, transposed_out=transposed_out)
    return pl.pallas_call(
        kern,
        grid=(batch, gdim),
        in_specs=[
            pl.BlockSpec((1, 1, rows, feat), lambda b, g: (b, g, 0, 0)),
            pl.BlockSpec(pos8.shape, lambda b, g: (0, 0)),
            pl.BlockSpec(w1.shape, lambda b, g: (0, 0)),
            pl.BlockSpec(w1cat.shape, lambda b, g: (0, 0)),
            pl.BlockSpec(w2.shape, lambda b, g: (0, 0)),
        ],
        out_specs=out_spec,
        out_shape=out_shape,
        compiler_params=_cparams(2),
        name="compress_v" if transposed_out else "compress_k",
    )(x4, pos8, w1, w1cat, w2)


def _nsa_q_proj_kernel(x_ref, g_ref, wq_ref, wg_ref, cos_ref, sin_ref, q_ref, gate_ref):
    hn = (_rms_scale(x_ref[...]) * g_ref[...]).astype(BF)
    tm = hn.shape[0]
    qt = _dot_nt(wq_ref[...], hn) * (NSA_HD ** -0.5 * LOG2E)
    cos, sin = cos_ref[...], sin_ref[...]
    for h in range(NSA_HEADS):
        g, r = divmod(h, NSA_REP)
        base = h * NSA_HD
        x1 = qt[base:base + ROPE_HALF]
        x2 = qt[base + ROPE_HALF:base + ROPE_DIMS]
        head = jnp.concatenate(
            [x1 * cos - x2 * sin, x1 * sin + x2 * cos, qt[base + ROPE_DIMS:base + NSA_HD]], axis=0).astype(BF)
        for blk in range(tm // Q_BLOCK):
            col = blk * NSA_REP * Q_BLOCK + r * Q_BLOCK
            q_ref[0, g, :, col:col + Q_BLOCK] = head[:, blk * Q_BLOCK:(blk + 1) * Q_BLOCK]
    gt = jax.nn.sigmoid(_dot_nt(wg_ref[...], hn))
    for g in range(NSA_GROUPS):
        gate_ref[0, g] = gt[g * 16:(g + 1) * 16]


def _nsa_q_proj(x2d, g, wqt, wgt, cos_t, sin_t, *, batch, seq, tm=512):
    n, d = x2d.shape
    tpos = seq // tm
    qcols = NSA_REP * Q_BLOCK
    return pl.pallas_call(
        _nsa_q_proj_kernel,
        grid=(n // tm,),
        in_specs=[
            pl.BlockSpec((tm, d), lambda i: (i, 0)),
            pl.BlockSpec((1, d), lambda i: (0, 0)),
            pl.BlockSpec(wqt.shape, lambda i: (0, 0)),
            pl.BlockSpec(wgt.shape, lambda i: (0, 0)),
            pl.BlockSpec((ROPE_HALF, tm), lambda i: (0, i % tpos)),
            pl.BlockSpec((ROPE_HALF, tm), lambda i: (0, i % tpos)),
        ],
        out_specs=[
            pl.BlockSpec((1, NSA_GROUPS, NSA_HD, (tm // Q_BLOCK) * qcols), lambda i: (i // tpos, 0, 0, i % tpos)),
            pl.BlockSpec((1, NSA_GROUPS, 16, tm), lambda i: (i // tpos, 0, 0, i % tpos)),
        ],
        out_shape=[
            jax.ShapeDtypeStruct((batch, NSA_GROUPS, NSA_HD, (seq // Q_BLOCK) * qcols), BF),
            jax.ShapeDtypeStruct((batch, NSA_GROUPS, 16, seq), F32),
        ],
        compiler_params=_cparams(1),
        name="nsa_q_proj",
    )(x2d, g, wqt, wgt, cos_t, sin_t)


def _softmax_cols(s):
    m = jnp.max(s, axis=0, keepdims=True)
    p = jnp.exp2(s - m)
    return p, jnp.sum(p, axis=0, keepdims=True)


def _nsa_attn_kernel(q_ref, gate_ref, ck_ref, cvt_ref, ks_ref, vst_ref, kw_ref, vwt_ref, o_ref,
                     mt_ref, m_ref, acc_ref, qa_ref, sc_ref):
    gp = q_ref.shape[1]
    bi = pl.program_id(2)
    qcols = NSA_REP * Q_BLOCK
    lane = lax.broadcasted_iota(jnp.int32, (1, qcols), 1)
    t = bi * Q_BLOCK + (lane & (Q_BLOCK - 1))

    n_rows = ck_ref.shape[2]
    n_slc = n_rows // 4
    rho = lax.broadcasted_iota(jnp.int32, (n_rows, 1), 0)
    slc_bits = n_slc.bit_length() - 1
    n_idx = 4 * (rho & (n_slc - 1)) + (rho >> slc_bits)
    valid = (n_idx * CMP_STRIDE + (L_CMP - 1)) <= t
    j_io = lax.broadcasted_iota(jnp.int32, (n_slc, Q_BLOCK), 0)
    w_keys = WINDOW + Q_BLOCK
    wb = jnp.maximum(bi - WINDOW // Q_BLOCK, 0)
    start = pl.multiple_of(wb * Q_BLOCK, Q_BLOCK)
    s_cmp = [_dot(ck_ref[0, g], q_ref[0, g]) for g in range(gp)]
    s_win = [_dot(kw_ref[0, g, pl.ds(start, w_keys), :], q_ref[0, g]) for g in range(gp)]
    oc, imps = [], []
    for g in range(gp):
        p, l = _softmax_cols(jnp.where(valid, s_cmp[g], NEG))
        pc = p * jnp.where(t >= L_CMP - 1, 1.0 / l, 0.0)
        oc.append(_dot(cvt_ref[0, g], pc.astype(BF)))
        ps = pc[:, 0:Q_BLOCK]
        for r in range(1, NSA_REP):
            ps = ps + pc[:, r * Q_BLOCK:(r + 1) * Q_BLOCK]
        p0, p1, p2, p3 = (ps[k * n_slc:(k + 1) * n_slc] for k in range(4))
        p3_prev = jnp.where(j_io == 0, 0.0, pltpu.roll(p3, 1, axis=0))
        imps.append(p3_prev + 2.0 * p0 + 2.0 * p1 + 2.0 * p2 + p3)
    imp = jnp.concatenate(imps, axis=1) if gp > 1 else imps[0]

    j_w = lax.broadcasted_iota(jnp.int32, imp.shape, 0)
    tq = bi * Q_BLOCK + (lax.broadcasted_iota(jnp.int32, (1, imp.shape[1]), 1) & (Q_BLOCK - 1))
    cur = tq >> (L_SLC.bit_length() - 1)
    forced = (j_w == 0) | (j_w == cur) | (j_w == cur - 1)
    val = jnp.where(forced, TAKEN, jnp.where(j_w <= cur, imp, NEG))
    j_f = j_w.astype(F32)
    for _ in range(min(N_SELECT, n_slc) - 3):
        mx = jnp.max(val, axis=0, keepdims=True)
        idx = jnp.min(jnp.where(val == mx, j_f, float(n_slc)), axis=0, keepdims=True)
        val = jnp.where(j_f == idx, TAKEN, val)
    mt = jnp.where(val < 0.5 * TAKEN, 0.0, NEG)
    for g in range(gp):
        for r in range(NSA_REP):
            mt_ref[g, :, r * Q_BLOCK:(r + 1) * Q_BLOCK] = mt[:, g * Q_BLOCK:(g + 1) * Q_BLOCK]

    dist = t - (start + lax.broadcasted_iota(jnp.int32, (w_keys, 1), 0))
    in_window = (dist >= 0) & (dist < WINDOW)
    ow = []
    for g in range(gp):
        pw, lw = _softmax_cols(jnp.where(in_window, s_win[g], NEG))
        vw = jnp.concatenate([vwt_ref[0, g, wb + w] for w in range(w_keys // LANES)], axis=1)
        ow.append(_dot(vw, pw.astype(BF)) * (1.0 / lw))

    for g in range(gp):
        m_ref[g] = jnp.full((1, qcols), NEG, F32)
        acc_ref[g] = jnp.zeros((SEL_VROWS, qcols), F32)
        qa_ref[g, 0:NSA_HD, :] = q_ref[0, g]
        qa_ref[g, NSA_HD:, :] = jnp.zeros((LANES - NSA_HD, qcols), BF)
    row_io = lax.broadcasted_iota(jnp.int32, (SEL_KEYS, 1), 0)
    pad_rows = jnp.zeros((16 - SEL_BLOCKS, qcols), BF)

    def sel_scores(c, slot, g):
        bias = mt_ref[g, pl.ds(pl.multiple_of(c * SEL_BLOCKS, SEL_BLOCKS), SEL_BLOCKS), :].astype(BF)
        qa_ref[g, NSA_HD:NSA_HD + 16, :] = jnp.concatenate([bias, pad_rows], axis=0)
        k = ks_ref[0, g, pl.ds(pl.multiple_of(c * SEL_KEYS, SEL_KEYS), SEL_KEYS), :]
        sc_ref[slot, g] = _dot(k, qa_ref[g])

    def sel_accumulate(c, slot, causal, g):
        sc = sc_ref[slot, g]
        if causal:
            sc = jnp.where((c * SEL_KEYS + row_io) <= t, sc, NEG)
        m_old = m_ref[g]
        m_new = jnp.maximum(m_old, jnp.max(sc, axis=0, keepdims=True))
        alpha = jnp.exp2(m_old - m_new)
        pr = jnp.exp2(sc - m_new).astype(BF)
        acc_ref[g] = alpha * acc_ref[g] + _dot(vst_ref[0, g, c], pr)
        m_ref[g] = m_new

    def sel_pair(k, carry):
        c = 2 * k
        for g in range(gp):
            sel_scores(c + 1, 1, g)
            sel_accumulate(c, 0, False, g)
        for g in range(gp):
            sel_scores(c + 2, 0, g)
            sel_accumulate(c + 1, 1, False, g)
        return carry

    n_full = (bi * Q_BLOCK) >> (SEL_KEYS.bit_length() - 1)
    for g in range(gp):
        sel_scores(0, 0, g)
    lax.fori_loop(0, n_full >> 1, sel_pair, 0)

    @pl.when((n_full & 1) == 0)
    def _():
        for g in range(gp):
            sel_accumulate(n_full, 0, True, g)

    @pl.when((n_full & 1) == 1)
    def _():
        for g in range(gp):
            sel_scores(n_full, 1, g)
            sel_accumulate(n_full - 1, 0, False, g)
        for g in range(gp):
            sel_accumulate(n_full, 1, True, g)

    out_w = NSA_REP * NSA_HD
    for g in range(gp):
        osel = acc_ref[g, 0:NSA_HD, :] * (1.0 / acc_ref[g, NSA_HD:NSA_HD + 1, :])
        heads = []
        for r in range(NSA_REP):
            sl = slice(r * Q_BLOCK, (r + 1) * Q_BLOCK)
            gc = gate_ref[0, g, 0 * NSA_REP + r:0 * NSA_REP + r + 1, :]
            gs = gate_ref[0, g, 1 * NSA_REP + r:1 * NSA_REP + r + 1, :]
            gw = gate_ref[0, g, 2 * NSA_REP + r:2 * NSA_REP + r + 1, :]
            heads.append(gc * oc[g][:, sl] + gs * osel[:, sl] + gw * ow[g][:, sl])
        o_ref[0, :, g * out_w:(g + 1) * out_w] = jnp.concatenate(heads, axis=0).T.astype(BF)


def _nsa_attn(qt, gates, ck, cvt, ks, vst, kw, vwt, *, batch, seq, gp=2):
    nb = seq // Q_BLOCK
    qcols = NSA_REP * Q_BLOCK
    n_cmp_rows = ck.shape[2]
    hd = NSA_HD
    full = lambda shape: pl.BlockSpec((1, gp) + tuple(shape[2:]), lambda b, g, i: (b, g) + (0,) * (len(shape) - 2))
    return pl.pallas_call(
        _nsa_attn_kernel,
        grid=(batch, NSA_GROUPS // gp, nb),
        in_specs=[
            pl.BlockSpec((1, gp, hd, qcols), lambda b, g, i: (b, g, 0, i)),
            pl.BlockSpec((1, gp, 16, Q_BLOCK), lambda b, g, i: (b, g, 0, i)),
            full(ck.shape), full(cvt.shape), full(ks.shape), full(vst.shape), full(kw.shape), full(vwt.shape),
        ],
        out_specs=pl.BlockSpec((1, Q_BLOCK, gp * NSA_REP * hd), lambda b, g, i: (b, i, g)),
        out_shape=jax.ShapeDtypeStruct((batch, seq, NSA_HEADS * hd), BF),
        scratch_shapes=[
            pltpu.VMEM((gp, n_cmp_rows // 4, qcols), F32),
            pltpu.VMEM((gp, 1, qcols), F32),
            pltpu.VMEM((gp, SEL_VROWS, qcols), F32),
            pltpu.VMEM((gp, LANES, qcols), BF),
            pltpu.VMEM((2, gp, SEL_KEYS, qcols), F32),
        ],
        compiler_params=_cparams(3),
        name="nsa_attn",
    )(qt, gates, ck, cvt, ks, vst, kw, vwt)


def _retention_tables(seq):
    c = RET_CHUNK
    freqs = 1.0 / (RET_ROT_BASE ** jnp.linspace(0.0, 1.0, RET_DK // 2, dtype=F32))
    ang = jnp.arange(seq, dtype=F32)[:, None] * freqs[None, :]
    log_g = jnp.log1p(-jnp.exp2(-5.0 - jnp.arange(RET_HEADS, dtype=F32)))
    idx = jnp.arange(c, dtype=F32)
    diff = idx[:, None] - idx[None, :]
    decay = jnp.where(diff >= 0, jnp.exp(log_g[:, None, None] * jnp.maximum(diff, 0.0)), 0.0)
    qd = jnp.exp(log_g[:, None] * (idx + 1.0))[:, :, None]
    kd = jnp.exp(log_g[:, None] * (c - 1.0 - idx))[:, :, None]
    cd = jnp.exp(log_g * c)[:, None, None]
    return (jnp.cos(ang), jnp.sin(ang), decay,
            jnp.broadcast_to(qd, (RET_HEADS, c, RET_DV)),
            jnp.broadcast_to(kd, (RET_HEADS, c, RET_DK)),
            jnp.broadcast_to(cd, (RET_HEADS, 1, RET_DV)))


def _rope_tables(seq):
    freqs = ROPE_THETA ** (-jnp.arange(0, ROPE_DIMS, 2, dtype=F32) / ROPE_DIMS)
    ang = jnp.arange(seq, dtype=F32)[:, None] * freqs[None, :]
    cos, sin = jnp.cos(ang), jnp.sin(ang)
    ones = jnp.ones((seq, NSA_HD - ROPE_DIMS), F32)
    zeros8 = jnp.zeros((seq, ROPE_HALF), F32)
    zeros_rest = jnp.zeros((seq, NSA_HD - ROPE_DIMS), F32)
    ctab = jnp.concatenate([cos, cos, ones], axis=1)
    s1 = jnp.concatenate([zeros8, sin, zeros_rest], axis=1)
    s2 = jnp.concatenate([-sin, zeros8, zeros_rest], axis=1)
    two = lambda a: jnp.concatenate([a, a], axis=1)
    return two(ctab), two(s1), two(s2), cos.T, sin.T


def kernel(x, norms, ret_w_in, ret_gn, ret_w_out, nsa_w_in, nsa_w_out, kv_norm, w_kv, cmp_pos_k, cmp_pos_v,
           cmp_w1_k, cmp_w2_k, cmp_w1_v, cmp_w2_v, ffn_w_in, ffn_w_out):
    batch, seq, d = x.shape
    n = batch * seq
    x2d = x.reshape(n, d)
    norm = lambda layer, k: norms[layer, k][None, :]

    cos_r, sin_r, decay, qd, kd, cd = _retention_tables(seq)
    w3 = ret_w_in[0].reshape(d, 3, -1).transpose(1, 0, 2).astype(BF)
    proj = _ret_proj(x2d, norm(0, 0), w3, cos_r, sin_r, seq=seq)
    y = _retention(proj, decay, qd, kd, cd, ret_gn[0][None, :], batch=batch, seq=seq)
    x2d = _mix_ffn(y.reshape(n, -1), x2d, ret_w_out[0].astype(BF), norm(0, 1), norm(0, 2),
                   ffn_w_in[0].astype(BF), ffn_w_out[0].astype(BF), norm(0, 3))

    gw = NSA_GROUPS * NSA_HD
    sec = lambda s: w_kv[:, s * gw:(s + 1) * gw]
    wn = jnp.concatenate([sec(0), sec(1), sec(2), sec(4)], axis=1).astype(BF)
    wt = jnp.concatenate([sec(3), sec(5)], axis=1).T.astype(BF)
    ctab, s1tab, s2tab, cos_t, sin_t = _rope_tables(seq)
    k_cmp, v_cmp, k_slc, k_win, v_slc_t, v_win_t = _kv_proj(
        x2d, kv_norm[None, :], wn, wt, ctab, s1tab, s2tab, batch=batch, seq=seq)

    half = CMP_STRIDE * NSA_HD
    rows = seq // (4 * CMP_STRIDE)

    def compress(a, pos, w1, w2, transposed_out):
        x4 = a.reshape(batch, NSA_GROUPS, rows, 4 * half)
        pos8 = jnp.broadcast_to(pos.reshape(1, L_CMP * NSA_HD), (8, L_CMP * NSA_HD)).astype(BF)
        w1cat = jnp.concatenate([w1[:half], w1[half:]], axis=1).astype(BF)
        w2p = (w2.T if transposed_out else w2).astype(BF)
        return _compress(x4, pos8, w1.astype(BF), w1cat, w2p, transposed_out=transposed_out)

    ck = compress(k_cmp, cmp_pos_k, cmp_w1_k, cmp_w2_k, False)
    cvt = compress(v_cmp, cmp_pos_v, cmp_w1_v, cmp_w2_v, True)

    hq = NSA_HEADS * NSA_HD
    wqt = nsa_w_in[0][:, :hq].T.astype(BF)
    wg = nsa_w_in[0][:, hq:].reshape(d, NSA_GROUPS, NSA_REP, 3).transpose(0, 1, 3, 2)
    wg = jnp.pad(wg.reshape(d, NSA_GROUPS, 3 * NSA_REP), ((0, 0), (0, 0), (0, 16 - 3 * NSA_REP)))
    wgt = wg.reshape(d, NSA_GROUPS * 16).T.astype(BF)
    qt, gates = _nsa_q_proj(x2d, norm(1, 0), wqt, wgt, cos_t, sin_t, batch=batch, seq=seq)
    o = _nsa_attn(qt, gates, ck, cvt, k_slc, v_slc_t, k_win, v_win_t, batch=batch, seq=seq)
    x2d = _mix_ffn(o.reshape(n, -1), x2d, nsa_w_out[0].astype(BF), norm(1, 1), norm(1, 2),
                   ffn_w_in[1].astype(BF), ffn_w_out[1].astype(BF), norm(1, 3))
    return x2d.reshape(batch, seq, d)
```

```python
import functools

import jax
import jax.numpy as jnp
import numpy as np
from jax import lax
from jax.experimental import pallas as pl
from jax.experimental.pallas import tpu as pltpu

BF = jnp.bfloat16
F32 = jnp.float32

D_MODEL = 1024
RET_HEADS = 4
RET_DK = 256
RET_DV = 512
RET_CHUNK = 128
RET_ROT_BASE = 10000.0
NSA_HEADS = 16
NSA_GROUPS = 4
NSA_HD = 64
NSA_REP = 4
L_CMP = 32
CMP_STRIDE = 16
L_SLC = 64
N_SELECT = 16
WINDOW = 512
CMP_HIDDEN = 256
Q_BLOCK = 128
ROPE_THETA = 500000.0
ROPE_DIMS = 16
ROPE_HALF = ROPE_DIMS // 2
D_FF = 2816
EPS = 1e-6
NEG = -1e30
TAKEN = -3e38

LANES = 128
SEL_KEYS = 512
SEL_BLOCKS = SEL_KEYS // L_SLC
SEL_VROWS = NSA_HD + 16
LOG2E = 1.4426950408889634
VMEM_LIMIT = 48 * 1024 * 1024
NSA_VMEM_LIMIT = 58 * 1024 * 1024


def _cparams(n_axes):
    return pltpu.CompilerParams(dimension_semantics=("arbitrary",) * n_axes,
                                vmem_limit_bytes=VMEM_LIMIT)


def _rms_scale(x):
    return x * lax.rsqrt(jnp.mean(x * x, axis=-1, keepdims=True) + EPS)


def _dot(a, b):
    return jnp.dot(a, b, preferred_element_type=F32)


def _dot_nt(a, b):
    return lax.dot_general(a, b, (((1,), (1,)), ((), ())), preferred_element_type=F32)


def _dot_tn(a, b):
    return lax.dot_general(a, b, (((0,), (0,)), ((), ())), preferred_element_type=F32)


def _ret_proj_kernel(x_ref, g_ref, w_ref, cos_ref, sin_ref, o_ref, hn_ref):
    j = pl.program_id(1)
    half = RET_DK // 2

    @pl.when(j == 0)
    def _():
        hn_ref[...] = (_rms_scale(x_ref[...]) * g_ref[...]).astype(BF)
        cos = cos_ref[...]
        sin = sin_ref[...]
        for h in range(2 * RET_HEADS):
            acc = _dot(hn_ref[...], w_ref[0, :, h * RET_DK:(h + 1) * RET_DK])
            scale = RET_DK ** -0.5 if h >= RET_HEADS else 1.0
            x1 = acc[:, :half]
            x2 = acc[:, half:]
            o_ref[:, h * RET_DK:h * RET_DK + half] = ((x1 * cos - x2 * sin) * scale).astype(BF)
            o_ref[:, h * RET_DK + half:(h + 1) * RET_DK] = ((x1 * sin + x2 * cos) * scale).astype(BF)

    @pl.when(j > 0)
    def _():
        step = 2 * RET_DK
        for c in range(o_ref.shape[1] // step):
            o_ref[:, c * step:(c + 1) * step] = _dot(hn_ref[...], w_ref[0, :, c * step:(c + 1) * step]).astype(BF)


def _ret_proj(x2d, g, w3, cos, sin, *, seq, tm=1024):
    n, d = x2d.shape
    n_tiles, _, tn = w3.shape
    tpos = seq // tm
    return pl.pallas_call(
        _ret_proj_kernel,
        grid=(n // tm, n_tiles),
        in_specs=[
            pl.BlockSpec((tm, d), lambda i, j: (i, 0)),
            pl.BlockSpec((1, d), lambda i, j: (0, 0)),
            pl.BlockSpec((1, d, tn), lambda i, j: (j, 0, 0)),
            pl.BlockSpec((tm, RET_DK // 2), lambda i, j: (i % tpos, 0)),
            pl.BlockSpec((tm, RET_DK // 2), lambda i, j: (i % tpos, 0)),
        ],
        out_specs=pl.BlockSpec((tm, tn), lambda i, j: (i, j)),
        out_shape=jax.ShapeDtypeStruct((n, n_tiles * tn), BF),
        scratch_shapes=[pltpu.VMEM((tm, d), BF)],
        compiler_params=_cparams(2),
        name="ret_proj",
    )(x2d, g, w3, cos, sin)


def _retention_kernel(q_ref, k_ref, v_ref, g_ref, decay_ref, qd_ref, kd_ref, cd_ref, gn_ref, o_ref, state_ref):
    c = pl.program_id(1)

    @pl.when(c == 0)
    def _():
        state_ref[...] = jnp.zeros_like(state_ref)

    for h in range(RET_HEADS):
        ks = slice(h * RET_DK, (h + 1) * RET_DK)
        vs = slice(h * RET_DV, (h + 1) * RET_DV)
        q = q_ref[0, :, ks]
        k = k_ref[0, :, ks]
        v = v_ref[0, :, vs]
        scores = _dot_nt(q, k) * decay_ref[h]
        inner = _dot(scores.astype(BF), v)
        state = state_ref[h]
        cross = _dot(q, state.astype(BF)) * qd_ref[h]
        kk = (k.astype(F32) * kd_ref[h]).astype(BF)
        state_ref[h] = state * cd_ref[h] + _dot_tn(kk, v)
        o = inner + cross
        mu = jnp.mean(o, axis=-1, keepdims=True)
        dlt = o - mu
        var = jnp.mean(dlt * dlt, axis=-1, keepdims=True)
        on = dlt * lax.rsqrt(var + EPS) * gn_ref[:, vs]
        gate = g_ref[0, :, vs].astype(F32)
        o_ref[0, :, vs] = (gate * jax.nn.sigmoid(gate) * on).astype(BF)


def _retention(proj, decay, qd, kd, cd, gn, *, batch, seq):
    c = RET_CHUNK
    qk_w = RET_HEADS * RET_DK
    v_w = RET_HEADS * RET_DV
    proj3 = proj.reshape(batch, seq, proj.shape[-1])
    const = lambda a: pl.BlockSpec(a.shape, lambda b, i: (0,) * a.ndim)
    return pl.pallas_call(
        _retention_kernel,
        grid=(batch, seq // c),
        in_specs=[
            pl.BlockSpec((1, c, qk_w), lambda b, i: (b, i, 0)),
            pl.BlockSpec((1, c, qk_w), lambda b, i: (b, i, 1)),
            pl.BlockSpec((1, c, v_w), lambda b, i: (b, i, 1)),
            pl.BlockSpec((1, c, v_w), lambda b, i: (b, i, 2)),
            const(decay), const(qd), const(kd), const(cd), const(gn),
        ],
        out_specs=pl.BlockSpec((1, c, v_w), lambda b, i: (b, i, 0)),
        out_shape=jax.ShapeDtypeStruct((batch, seq, v_w), BF),
        scratch_shapes=[pltpu.VMEM((RET_HEADS, RET_DK, RET_DV), F32)],
        compiler_params=_cparams(2),
        name="retention",
    )(proj3, proj3, proj3, proj3, decay, qd, kd, cd, gn)


FFN_CHUNKS = ((0, 1024), (1024, 2048), (2048, D_FF))


def _mix_ffn_kernel(y_ref, x_ref, wm_ref, gm_ref, g_in_ref, wi_ref, wo_ref, g_out_ref, o_ref):
    x1 = x_ref[...] + _rms_scale(_dot(y_ref[...], wm_ref[...])) * gm_ref[...]
    h = (_rms_scale(x1) * g_in_ref[...]).astype(BF)
    acc = None
    for lo, hi in FFN_CHUNKS:
        gate = _dot(h, wi_ref[:, lo:hi])
        up = _dot(h, wi_ref[:, D_FF + lo:D_FF + hi])
        act = (gate * jax.nn.sigmoid(gate) * up).astype(BF)
        part = _dot(act, wo_ref[lo:hi, :])
        acc = part if acc is None else acc + part
    o_ref[...] = x1 + _rms_scale(acc) * g_out_ref[...]


def _mix_ffn(y2d, x2d, w_mix, g_mix, g_in, w_in, w_out, g_out, *, tm=512):
    n, d = x2d.shape
    kdim = y2d.shape[1]
    resident = lambda a: pl.BlockSpec(a.shape, lambda i: (0,) * a.ndim, pipeline_mode=pl.Buffered(1))
    row = lambda w: pl.BlockSpec((tm, w), lambda i: (i, 0))
    return pl.pallas_call(
        _mix_ffn_kernel,
        grid=(n // tm,),
        in_specs=[row(kdim), row(d), resident(w_mix), resident(g_mix), resident(g_in),
                  resident(w_in), resident(w_out), resident(g_out)],
        out_specs=row(d),
        out_shape=jax.ShapeDtypeStruct((n, d), F32),
        compiler_params=_cparams(1),
        name="mix_ffn",
    )(y2d, x2d, w_mix, g_mix, g_in, w_in, w_out, g_out)


def _kv_proj_kernel(x_ref, g_ref, wn_ref, wt_ref, c_ref, s1_ref, s2_ref,
                    kc_ref, vc_ref, ks_ref, kw_ref, vst_ref, vwt_ref):
    hn = (_rms_scale(x_ref[...]) * g_ref[...]).astype(BF)
    tm = hn.shape[0]
    gw = NSA_GROUPS * NSA_HD
    rn = _dot(hn, wn_ref[...])
    cc, s1, s2 = c_ref[...], s1_ref[...], s2_ref[...]
    lane = lax.broadcasted_iota(jnp.int32, (tm, LANES), 1)
    row = lax.broadcasted_iota(jnp.int32, (tm, LANES), 0)
    blk = (row >> (L_SLC.bit_length() - 1)) & (SEL_BLOCKS - 1)
    indicator = jnp.where(lane == NSA_HD + blk, 1.0, 0.0)
    outs = (kc_ref, vc_ref, ks_ref, kw_ref)
    rotate = (True, False, True, True)
    for sec in range(4):
        for pair in range(gw // LANES):
            col = sec * gw + pair * LANES
            x = rn[:, col:col + LANES]
            if rotate[sec]:
                x = (x * cc + pltpu.roll(x, ROPE_HALF, axis=1) * s1
                     + pltpu.roll(x, LANES - ROPE_HALF, axis=1) * s2)
            if sec == 2:
                outs[sec][0, 2 * pair] = jnp.where(lane < NSA_HD, x, indicator).astype(BF)
                outs[sec][0, 2 * pair + 1] = jnp.where(lane < NSA_HD, pltpu.roll(x, NSA_HD, axis=1), indicator).astype(BF)
            else:
                xb = x.astype(BF)
                outs[sec][0, 2 * pair] = xb[:, :NSA_HD]
                outs[sec][0, 2 * pair + 1] = xb[:, NSA_HD:]
    rt = _dot_nt(wt_ref[...], hn).astype(BF)
    ones = jnp.ones((SEL_VROWS - NSA_HD, SEL_KEYS), BF)
    for g in range(NSA_GROUPS):
        for cb in range(tm // SEL_KEYS):
            vst_ref[0, g, cb, 0:NSA_HD, :] = rt[g * NSA_HD:(g + 1) * NSA_HD, cb * SEL_KEYS:(cb + 1) * SEL_KEYS]
            vst_ref[0, g, cb, NSA_HD:SEL_VROWS, :] = ones
        for cb in range(tm // LANES):
            vwt_ref[0, g, cb, 0:NSA_HD, :] = rt[gw + g * NSA_HD:gw + (g + 1) * NSA_HD, cb * LANES:(cb + 1) * LANES]
            vwt_ref[0, g, cb, NSA_HD:SEL_VROWS, :] = ones[:, :LANES]


def _kv_proj(x2d, g, wn, wt, ctab, s1tab, s2tab, *, batch, seq, tm=512):
    n, d = x2d.shape
    tpos = seq // tm
    gdim, hd = NSA_GROUPS, NSA_HD
    normal = jax.ShapeDtypeStruct((batch, gdim, seq, hd), BF)
    normal_spec = pl.BlockSpec((1, gdim, tm, hd), lambda i: (i // tpos, 0, i % tpos, 0))
    tab_spec = pl.BlockSpec((tm, LANES), lambda i: (i % tpos, 0))
    return pl.pallas_call(
        _kv_proj_kernel,
        grid=(n // tm,),
        in_specs=[
            pl.BlockSpec((tm, d), lambda i: (i, 0)),
            pl.BlockSpec((1, d), lambda i: (0, 0)),
            pl.BlockSpec(wn.shape, lambda i: (0, 0)),
            pl.BlockSpec(wt.shape, lambda i: (0, 0)),
            tab_spec, tab_spec, tab_spec,
        ],
        out_specs=[
            normal_spec, normal_spec,
            pl.BlockSpec((1, gdim, tm, LANES), lambda i: (i // tpos, 0, i % tpos, 0)),
            normal_spec,
            pl.BlockSpec((1, gdim, tm // SEL_KEYS, SEL_VROWS, SEL_KEYS), lambda i: (i // tpos, 0, i % tpos, 0, 0)),
            pl.BlockSpec((1, gdim, tm // LANES, SEL_VROWS, LANES), lambda i: (i // tpos, 0, i % tpos, 0, 0)),
        ],
        out_shape=[
            normal, normal, jax.ShapeDtypeStruct((batch, gdim, seq, LANES), BF), normal,
            jax.ShapeDtypeStruct((batch, gdim, seq // SEL_KEYS, SEL_VROWS, SEL_KEYS), BF),
            jax.ShapeDtypeStruct((batch, gdim, seq // LANES, SEL_VROWS, LANES), BF),
        ],
        compiler_params=_cparams(1),
        name="kv_proj",
    )(x2d, g, wn, wt, ctab, s1tab, s2tab)


def _gelu_tanh(x):
    return 0.5 * x * (1.0 + jnp.tanh(0.7978845608028654 * (x + 0.044715 * (x * x * x))))


def _compress_kernel(x_ref, pos_ref, w1_ref, w1cat_ref, w2_ref, o_ref, *, transposed_out):
    half = CMP_STRIDE * NSA_HD
    pb = _dot(pos_ref[...], w1_ref[...])[0:1]
    first, second = [], []
    for k in range(4):
        res = _dot(x_ref[0, 0, :, k * half:(k + 1) * half], w1cat_ref[...])
        first.append(res[:, :CMP_HIDDEN])
        second.append(res[:, CMP_HIDDEN:])
    rows = first[0].shape[0]
    for k in range(4):
        nxt = second[k + 1] if k < 3 else pltpu.roll(second[0], rows - 1, axis=0)
        hid = _gelu_tanh(first[k] + nxt + pb).astype(BF)
        if transposed_out:
            o_ref[0, 0, :, k * rows:(k + 1) * rows] = _dot_nt(w2_ref[...], hid).astype(BF)
        else:
            o_ref[0, 0, k * rows:(k + 1) * rows, :] = _dot(hid, w2_ref[...]).astype(BF)


def _compress(x4, pos8, w1, w1cat, w2, *, transposed_out):
    batch, gdim, rows, feat = x4.shape
    n_rows = 4 * rows
    if transposed_out:
        out_shape = jax.ShapeDtypeStruct((batch, gdim, NSA_HD, n_rows), BF)
        out_spec = pl.BlockSpec((1, 1, NSA_HD, n_rows), lambda b, g: (b, g, 0, 0))
    else:
        out_shape = jax.ShapeDtypeStruct((batch, gdim, n_rows, NSA_HD), BF)
        out_spec = pl.BlockSpec((1, 1, n_rows, NSA_HD), lambda b, g: (b, g, 0, 0))
    kern = functools.partial(_compress_kernel, transposed_out=transposed_out)
    return pl.pallas_call(
        kern,
        grid=(batch, gdim),
        in_specs=[
            pl.BlockSpec((1, 1, rows, feat), lambda b, g: (b, g, 0, 0)),
            pl.BlockSpec(pos8.shape, lambda b, g: (0, 0)),
            pl.BlockSpec(w1.shape, lambda b, g: (0, 0)),
            pl.BlockSpec(w1cat.shape, lambda b, g: (0, 0)),
            pl.BlockSpec(w2.shape, lambda b, g: (0, 0)),
        ],
        out_specs=out_spec,
        out_shape=out_shape,
        compiler_params=_cparams(2),
        name="compress_v" if transposed_out else "compress_k",
    )(x4, pos8, w1, w1cat, w2)


def _nsa_q_proj_kernel(x_ref, g_ref, wq_ref, wg_ref, cos_ref, sin_ref, q_ref, gate_ref):
    hn = (_rms_scale(x_ref[...]) * g_ref[...]).astype(BF)
    tm = hn.shape[0]
    qt = _dot_nt(wq_ref[...], hn) * (NSA_HD ** -0.5 * LOG2E)
    cos, sin = cos_ref[...], sin_ref[...]
    for h in range(NSA_HEADS):
        g, r = divmod(h, NSA_REP)
        base = h * NSA_HD
        x1 = qt[base:base + ROPE_HALF]
        x2 = qt[base + ROPE_HALF:base + ROPE_DIMS]
        head = jnp.concatenate(
            [x1 * cos - x2 * sin, x1 * sin + x2 * cos, qt[base + ROPE_DIMS:base + NSA_HD]], axis=0).astype(BF)
        for blk in range(tm // Q_BLOCK):
            col = blk * NSA_REP * Q_BLOCK + r * Q_BLOCK
            q_ref[0, g, :, col:col + Q_BLOCK] = head[:, blk * Q_BLOCK:(blk + 1) * Q_BLOCK]
    gt = jax.nn.sigmoid(_dot_nt(wg_ref[...], hn))
    for g in range(NSA_GROUPS):
        gate_ref[0, g] = gt[g * 16:(g + 1) * 16]


def _nsa_q_proj(x2d, g, wqt, wgt, cos_t, sin_t, *, batch, seq, tm=512):
    n, d = x2d.shape
    tpos = seq // tm
    qcols = NSA_REP * Q_BLOCK
    return pl.pallas_call(
        _nsa_q_proj_kernel,
        grid=(n // tm,),
        in_specs=[
            pl.BlockSpec((tm, d), lambda i: (i, 0)),
            pl.BlockSpec((1, d), lambda i: (0, 0)),
            pl.BlockSpec(wqt.shape, lambda i: (0, 0)),
            pl.BlockSpec(wgt.shape, lambda i: (0, 0)),
            pl.BlockSpec((ROPE_HALF, tm), lambda i: (0, i % tpos)),
            pl.BlockSpec((ROPE_HALF, tm), lambda i: (0, i % tpos)),
        ],
        out_specs=[
            pl.BlockSpec((1, NSA_GROUPS, NSA_HD, (tm // Q_BLOCK) * qcols), lambda i: (i // tpos, 0, 0, i % tpos)),
            pl.BlockSpec((1, NSA_GROUPS, 16, tm), lambda i: (i // tpos, 0, 0, i % tpos)),
        ],
        out_shape=[
            jax.ShapeDtypeStruct((batch, NSA_GROUPS, NSA_HD, (seq // Q_BLOCK) * qcols), BF),
            jax.ShapeDtypeStruct((batch, NSA_GROUPS, 16, seq), F32),
        ],
        compiler_params=_cparams(1),
        name="nsa_q_proj",
    )(x2d, g, wqt, wgt, cos_t, sin_t)


def _nsa_attn_kernel(q_ref, gate_ref, ck_ref, cl_ref, ks_ref, vst_ref, kw_ref, vwt_ref, o_ref,
                     mt_ref, m_ref, acc_ref, qa_ref, sc_ref):
    gp = q_ref.shape[1]
    bi = pl.program_id(2)
    qcols = NSA_REP * Q_BLOCK
    lane = lax.broadcasted_iota(jnp.int32, (1, qcols), 1)
    t = bi * Q_BLOCK + (lane & (Q_BLOCK - 1))

    n_rows = ck_ref.shape[2]
    n_slc = n_rows // 4
    rho = lax.broadcasted_iota(jnp.int32, (n_rows, 1), 0)
    slc_bits = n_slc.bit_length() - 1
    n_idx = 4 * (rho & (n_slc - 1)) + (rho >> slc_bits)
    valid = (n_idx * CMP_STRIDE + (L_CMP - 1)) <= t
    w_keys = WINDOW + Q_BLOCK
    wb = jnp.maximum(bi - WINDOW // Q_BLOCK, 0)
    start = pl.multiple_of(wb * Q_BLOCK, Q_BLOCK)
    s_cmp = [_dot(ck_ref[0, g], q_ref[0, g]) for g in range(gp)]
    s_win = [_dot(kw_ref[0, g, pl.ds(start, w_keys), :], q_ref[0, g]) for g in range(gp)]
    oc, imps = [], []
    for g in range(gp):
        s = jnp.where(valid, s_cmp[g], NEG)
        p = jnp.exp2(s - jnp.max(s, axis=0, keepdims=True)).astype(BF)
        res = _dot(cl_ref[0, g], p)
        inv = jnp.where(t >= L_CMP - 1, 1.0 / res[NSA_HD:NSA_HD + 1], 0.0)
        oc.append(res[0:NSA_HD] * inv)
        impu = res[SEL_VROWS:] * inv
        imp_g = impu[:, 0:Q_BLOCK]
        for r in range(1, NSA_REP):
            imp_g = imp_g + impu[:, r * Q_BLOCK:(r + 1) * Q_BLOCK]
        imps.append(imp_g)
    imp = jnp.concatenate(imps, axis=1) if gp > 1 else imps[0]

    j_w = lax.broadcasted_iota(jnp.int32, imp.shape, 0)
    tq = bi * Q_BLOCK + (lax.broadcasted_iota(jnp.int32, (1, imp.shape[1]), 1) & (Q_BLOCK - 1))
    cur = tq >> (L_SLC.bit_length() - 1)
    forced = (j_w == 0) | (j_w == cur) | (j_w == cur - 1)
    val = jnp.where(forced, TAKEN, jnp.where(j_w <= cur, imp, NEG))
    j_f = j_w.astype(F32)
    for _ in range(min(N_SELECT, n_slc) - 3):
        mx = jnp.max(val, axis=0, keepdims=True)
        idx = jnp.min(jnp.where(val == mx, j_f, float(n_slc)), axis=0, keepdims=True)
        val = jnp.where(j_f == idx, TAKEN, val)
    mt = jnp.where(val < 0.5 * TAKEN, 0.0, NEG)
    for g in range(gp):
        for r in range(NSA_REP):
            mt_ref[g, :, r * Q_BLOCK:(r + 1) * Q_BLOCK] = mt[:, g * Q_BLOCK:(g + 1) * Q_BLOCK]

    dist = t - (start + lax.broadcasted_iota(jnp.int32, (w_keys, 1), 0))
    in_window = lax.bitcast_convert_type(dist, jnp.uint32) < jnp.uint32(WINDOW)
    ow = []
    for g in range(gp):
        sw = jnp.where(in_window, s_win[g], NEG)
        pw = jnp.exp2(sw - jnp.max(sw, axis=0, keepdims=True)).astype(BF)
        vw = jnp.concatenate([vwt_ref[0, g, wb + w] for w in range(w_keys // LANES)], axis=1)
        res = _dot(vw, pw)
        ow.append(res[0:NSA_HD] * (1.0 / res[NSA_HD:NSA_HD + 1]))

    for g in range(gp):
        m_ref[g] = jnp.full((1, qcols), NEG, F32)
        acc_ref[g] = jnp.zeros((SEL_VROWS, qcols), F32)
        qa_ref[g, 0:NSA_HD, :] = q_ref[0, g]
        qa_ref[g, NSA_HD:, :] = jnp.zeros((LANES - NSA_HD, qcols), BF)
    row_io = lax.broadcasted_iota(jnp.int32, (SEL_KEYS, 1), 0)
    pad_rows = jnp.zeros((16 - SEL_BLOCKS, qcols), BF)

    def sel_scores(c, slot, g):
        bias = mt_ref[g, pl.ds(pl.multiple_of(c * SEL_BLOCKS, SEL_BLOCKS), SEL_BLOCKS), :].astype(BF)
        qa_ref[g, NSA_HD:NSA_HD + 16, :] = jnp.concatenate([bias, pad_rows], axis=0)
        k = ks_ref[0, g, pl.ds(pl.multiple_of(c * SEL_KEYS, SEL_KEYS), SEL_KEYS), :]
        sc_ref[slot, g] = _dot(k, qa_ref[g])

    def sel_accumulate(c, slot, causal, g):
        sc = sc_ref[slot, g]
        if causal:
            sc = jnp.where((c * SEL_KEYS + row_io) <= t, sc, NEG)
        m_old = m_ref[g]
        m_new = jnp.maximum(m_old, jnp.max(sc, axis=0, keepdims=True))
        alpha = jnp.exp2(m_old - m_new)
        pr = jnp.exp2(sc - m_new).astype(BF)
        acc_ref[g] = alpha * acc_ref[g] + _dot(vst_ref[0, g, c], pr)
        m_ref[g] = m_new

    def sel_pair(k, carry):
        c = 2 * k
        for g in range(gp):
            sel_scores(c + 1, 1, g)
            sel_accumulate(c, 0, False, g)
        for g in range(gp):
            sel_scores(c + 2, 0, g)
            sel_accumulate(c + 1, 1, False, g)
        return carry

    n_full = (bi * Q_BLOCK) >> (SEL_KEYS.bit_length() - 1)
    for g in range(gp):
        sel_scores(0, 0, g)
    lax.fori_loop(0, n_full >> 1, sel_pair, 0)

    @pl.when((n_full & 1) == 0)
    def _():
        for g in range(gp):
            sel_accumulate(n_full, 0, True, g)

    @pl.when((n_full & 1) == 1)
    def _():
        for g in range(gp):
            sel_scores(n_full, 1, g)
            sel_accumulate(n_full - 1, 0, False, g)
        for g in range(gp):
            sel_accumulate(n_full, 1, True, g)

    out_w = NSA_REP * NSA_HD
    for g in range(gp):
        osel = acc_ref[g, 0:NSA_HD, :] * (1.0 / acc_ref[g, NSA_HD:NSA_HD + 1, :])
        heads = []
        for r in range(NSA_REP):
            sl = slice(r * Q_BLOCK, (r + 1) * Q_BLOCK)
            gc = gate_ref[0, g, 0 * NSA_REP + r:0 * NSA_REP + r + 1, :]
            gs = gate_ref[0, g, 1 * NSA_REP + r:1 * NSA_REP + r + 1, :]
            gw = gate_ref[0, g, 2 * NSA_REP + r:2 * NSA_REP + r + 1, :]
            heads.append(gc * oc[g][:, sl] + gs * osel[:, sl] + gw * ow[g][:, sl])
        o_ref[0, :, g * out_w:(g + 1) * out_w] = jnp.concatenate(heads, axis=0).T.astype(BF)


def _nsa_attn(qt, gates, ck, cmp_lhs, ks, vst, kw, vwt, *, batch, seq, gp=4):
    nb = seq // Q_BLOCK
    qcols = NSA_REP * Q_BLOCK
    n_cmp_rows = ck.shape[2]
    hd = NSA_HD
    full = lambda shape: pl.BlockSpec((1, gp) + tuple(shape[2:]), lambda b, g, i: (b, g) + (0,) * (len(shape) - 2),
                                      pipeline_mode=pl.Buffered(1))
    return pl.pallas_call(
        _nsa_attn_kernel,
        grid=(batch, NSA_GROUPS // gp, nb),
        in_specs=[
            pl.BlockSpec((1, gp, hd, qcols), lambda b, g, i: (b, g, 0, i)),
            pl.BlockSpec((1, gp, 16, Q_BLOCK), lambda b, g, i: (b, g, 0, i)),
            full(ck.shape), full(cmp_lhs.shape), full(ks.shape), full(vst.shape), full(kw.shape), full(vwt.shape),
        ],
        out_specs=pl.BlockSpec((1, Q_BLOCK, gp * NSA_REP * hd), lambda b, g, i: (b, i, g)),
        out_shape=jax.ShapeDtypeStruct((batch, seq, NSA_HEADS * hd), BF),
        scratch_shapes=[
            pltpu.VMEM((gp, n_cmp_rows // 4, qcols), F32),
            pltpu.VMEM((gp, 1, qcols), F32),
            pltpu.VMEM((gp, SEL_VROWS, qcols), F32),
            pltpu.VMEM((gp, LANES, qcols), BF),
            pltpu.VMEM((2, gp, SEL_KEYS, qcols), F32),
        ],
        compiler_params=pltpu.CompilerParams(dimension_semantics=("arbitrary",) * 3, vmem_limit_bytes=NSA_VMEM_LIMIT),
        name="nsa_attn",
    )(qt, gates, ck, cmp_lhs, ks, vst, kw, vwt)


def _retention_tables(seq):
    c = RET_CHUNK
    freqs = 1.0 / (RET_ROT_BASE ** jnp.linspace(0.0, 1.0, RET_DK // 2, dtype=F32))
    ang = jnp.arange(seq, dtype=F32)[:, None] * freqs[None, :]
    log_g = jnp.log1p(-jnp.exp2(-5.0 - jnp.arange(RET_HEADS, dtype=F32)))
    idx = jnp.arange(c, dtype=F32)
    diff = idx[:, None] - idx[None, :]
    decay = jnp.where(diff >= 0, jnp.exp(log_g[:, None, None] * jnp.maximum(diff, 0.0)), 0.0)
    qd = jnp.exp(log_g[:, None] * (idx + 1.0))[:, :, None]
    kd = jnp.exp(log_g[:, None] * (c - 1.0 - idx))[:, :, None]
    cd = jnp.exp(log_g * c)[:, None, None]
    return (jnp.cos(ang), jnp.sin(ang), decay,
            jnp.broadcast_to(qd, (RET_HEADS, c, RET_DV)),
            jnp.broadcast_to(kd, (RET_HEADS, c, RET_DK)),
            jnp.broadcast_to(cd, (RET_HEADS, 1, RET_DV)))


def _overlap_rows(n_slc):
    w = np.zeros((n_slc, 4 * n_slc), np.float32)
    j = np.arange(n_slc)
    for k in range(3):
        w[j, k * n_slc + j] = 2.0
    w[j, 3 * n_slc + j] = 1.0
    w[j[1:], 3 * n_slc + j[1:] - 1] = 1.0
    return jnp.asarray(w, BF)


def _rope_tables(seq):
    freqs = ROPE_THETA ** (-jnp.arange(0, ROPE_DIMS, 2, dtype=F32) / ROPE_DIMS)
    ang = jnp.arange(seq, dtype=F32)[:, None] * freqs[None, :]
    cos, sin = jnp.cos(ang), jnp.sin(ang)
    ones = jnp.ones((seq, NSA_HD - ROPE_DIMS), F32)
    zeros8 = jnp.zeros((seq, ROPE_HALF), F32)
    zeros_rest = jnp.zeros((seq, NSA_HD - ROPE_DIMS), F32)
    ctab = jnp.concatenate([cos, cos, ones], axis=1)
    s1 = jnp.concatenate([zeros8, sin, zeros_rest], axis=1)
    s2 = jnp.concatenate([-sin, zeros8, zeros_rest], axis=1)
    two = lambda a: jnp.concatenate([a, a], axis=1)
    return two(ctab), two(s1), two(s2), cos.T, sin.T


def kernel(x, norms, ret_w_in, ret_gn, ret_w_out, nsa_w_in, nsa_w_out, kv_norm, w_kv, cmp_pos_k, cmp_pos_v,
           cmp_w1_k, cmp_w2_k, cmp_w1_v, cmp_w2_v, ffn_w_in, ffn_w_out):
    batch, seq, d = x.shape
    n = batch * seq
    x2d = x.reshape(n, d)
    norm = lambda layer, k: norms[layer, k][None, :]

    cos_r, sin_r, decay, qd, kd, cd = _retention_tables(seq)
    w3 = ret_w_in[0].reshape(d, 3, -1).transpose(1, 0, 2).astype(BF)
    proj = _ret_proj(x2d, norm(0, 0), w3, cos_r, sin_r, seq=seq)
    y = _retention(proj, decay, qd, kd, cd, ret_gn[0][None, :], batch=batch, seq=seq)
    x2d = _mix_ffn(y.reshape(n, -1), x2d, ret_w_out[0].astype(BF), norm(0, 1), norm(0, 2),
                   ffn_w_in[0].astype(BF), ffn_w_out[0].astype(BF), norm(0, 3))

    gw = NSA_GROUPS * NSA_HD
    sec = lambda s: w_kv[:, s * gw:(s + 1) * gw]
    wn = jnp.concatenate([sec(0), sec(1), sec(2), sec(4)], axis=1).astype(BF)
    wt = jnp.concatenate([sec(3), sec(5)], axis=1).T.astype(BF)
    ctab, s1tab, s2tab, cos_t, sin_t = _rope_tables(seq)
    k_cmp, v_cmp, k_slc, k_win, v_slc_t, v_win_t = _kv_proj(
        x2d, kv_norm[None, :], wn, wt, ctab, s1tab, s2tab, batch=batch, seq=seq)

    half = CMP_STRIDE * NSA_HD
    rows = seq // (4 * CMP_STRIDE)

    def compress(a, pos, w1, w2, transposed_out):
        x4 = a.reshape(batch, NSA_GROUPS, rows, 4 * half)
        pos8 = jnp.broadcast_to(pos.reshape(1, L_CMP * NSA_HD), (8, L_CMP * NSA_HD)).astype(BF)
        w1cat = jnp.concatenate([w1[:half], w1[half:]], axis=1).astype(BF)
        w2p = (w2.T if transposed_out else w2).astype(BF)
        return _compress(x4, pos8, w1.astype(BF), w1cat, w2p, transposed_out=transposed_out)

    ck = compress(k_cmp, cmp_pos_k, cmp_w1_k, cmp_w2_k, False)
    cvt = compress(v_cmp, cmp_pos_v, cmp_w1_v, cmp_w2_v, True)

    hq = NSA_HEADS * NSA_HD
    wqt = nsa_w_in[0][:, :hq].T.astype(BF)
    wg = nsa_w_in[0][:, hq:].reshape(d, NSA_GROUPS, NSA_REP, 3).transpose(0, 1, 3, 2)
    wg = jnp.pad(wg.reshape(d, NSA_GROUPS, 3 * NSA_REP), ((0, 0), (0, 0), (0, 16 - 3 * NSA_REP)))
    wgt = wg.reshape(d, NSA_GROUPS * 16).T.astype(BF)
    qt, gates = _nsa_q_proj(x2d, norm(1, 0), wqt, wgt, cos_t, sin_t, batch=batch, seq=seq)
    n_cmp_rows = cvt.shape[-1]
    cmp_lhs = jnp.concatenate(
        [cvt, jnp.ones((batch, NSA_GROUPS, SEL_VROWS - NSA_HD, n_cmp_rows), BF),
         jnp.broadcast_to(_overlap_rows(n_cmp_rows // 4), (batch, NSA_GROUPS, n_cmp_rows // 4, n_cmp_rows))],
        axis=2)
    o = _nsa_attn(qt, gates, ck, cmp_lhs, k_slc, v_slc_t, k_win, v_win_t, batch=batch, seq=seq)
    x2d = _mix_ffn(o.reshape(n, -1), x2d, nsa_w_out[0].astype(BF), norm(1, 1), norm(1, 2),
                   ffn_w_in[1].astype(BF), ffn_w_out[1].astype(BF), norm(1, 3))
    return x2d.reshape(batch, seq, d)
```

```python
import functools

import jax
import jax.numpy as jnp
import numpy as np
from jax import lax
from jax.experimental import pallas as pl
from jax.experimental.pallas import tpu as pltpu

BF = jnp.bfloat16
F32 = jnp.float32

D_MODEL = 1024
RET_HEADS = 4
RET_DK = 256
RET_DV = 512
RET_CHUNK = 128
RET_ROT_BASE = 10000.0
NSA_HEADS = 16
NSA_GROUPS = 4
NSA_HD = 64
NSA_REP = 4
L_CMP = 32
CMP_STRIDE = 16
L_SLC = 64
N_SELECT = 16
WINDOW = 512
CMP_HIDDEN = 256
Q_BLOCK = 128
ROPE_THETA = 500000.0
ROPE_DIMS = 16
ROPE_HALF = ROPE_DIMS // 2
D_FF = 2816
EPS = 1e-6
NEG = -1e30
TAKEN = -3e38

LANES = 128
SEL_KEYS = 512
SEL_BLOCKS = SEL_KEYS // L_SLC
SEL_VROWS = NSA_HD + 16
LOG2E = 1.4426950408889634
VMEM_LIMIT = 48 * 1024 * 1024
NSA_VMEM_LIMIT = 58 * 1024 * 1024


def _cparams(n_axes):
    return pltpu.CompilerParams(dimension_semantics=("arbitrary",) * n_axes,
                                vmem_limit_bytes=VMEM_LIMIT)


def _rms_scale(x):
    return x * lax.rsqrt(jnp.mean(x * x, axis=-1, keepdims=True) + EPS)


def _dot(a, b):
    return jnp.dot(a, b, preferred_element_type=F32)


def _dot_nt(a, b):
    return lax.dot_general(a, b, (((1,), (1,)), ((), ())), preferred_element_type=F32)


def _dot_tn(a, b):
    return lax.dot_general(a, b, (((0,), (0,)), ((), ())), preferred_element_type=F32)


def _ret_proj_kernel(x_ref, g_ref, w_ref, cos_ref, sin_ref, o_ref):
    half = RET_DK // 2
    hn = (_rms_scale(x_ref[...]) * g_ref[...]).astype(BF)
    cos = cos_ref[...]
    sin = sin_ref[...]
    for h in range(2 * RET_HEADS):
        acc = _dot(hn, w_ref[:, h * RET_DK:(h + 1) * RET_DK])
        scale = RET_DK ** -0.5 if h >= RET_HEADS else 1.0
        x1 = acc[:, :half]
        x2 = acc[:, half:]
        o_ref[:, h * RET_DK:h * RET_DK + half] = ((x1 * cos - x2 * sin) * scale).astype(BF)
        o_ref[:, h * RET_DK + half:(h + 1) * RET_DK] = ((x1 * sin + x2 * cos) * scale).astype(BF)
    step = 2 * RET_DK
    for c in range(2 * RET_HEADS * RET_DK // step, o_ref.shape[1] // step):
        o_ref[:, c * step:(c + 1) * step] = _dot(hn, w_ref[:, c * step:(c + 1) * step]).astype(BF)


def _ret_proj(x2d, g, w, cos, sin, *, seq, tm=512):
    n, d = x2d.shape
    nout = w.shape[1]
    tpos = seq // tm
    return pl.pallas_call(
        _ret_proj_kernel,
        grid=(n // tm,),
        in_specs=[
            pl.BlockSpec((tm, d), lambda i: (i, 0)),
            pl.BlockSpec((1, d), lambda i: (0, 0)),
            pl.BlockSpec((d, nout), lambda i: (0, 0), pipeline_mode=pl.Buffered(1)),
            pl.BlockSpec((tm, RET_DK // 2), lambda i: (i % tpos, 0)),
            pl.BlockSpec((tm, RET_DK // 2), lambda i: (i % tpos, 0)),
        ],
        out_specs=pl.BlockSpec((tm, nout), lambda i: (i, 0)),
        out_shape=jax.ShapeDtypeStruct((n, nout), BF),
        compiler_params=_cparams(1),
        name="ret_proj",
    )(x2d, g, w, cos, sin)


def _retention_kernel(q_ref, k_ref, v_ref, g_ref, decay_ref, qd_ref, kd_ref, cd_ref, gn_ref, o_ref, state_ref):
    c = pl.program_id(1)

    @pl.when(c == 0)
    def _():
        state_ref[...] = jnp.zeros_like(state_ref)

    for sub in range(q_ref.shape[1] // RET_CHUNK):
        rows = slice(sub * RET_CHUNK, (sub + 1) * RET_CHUNK)
        for h in range(RET_HEADS):
            ks = slice(h * RET_DK, (h + 1) * RET_DK)
            vs = slice(h * RET_DV, (h + 1) * RET_DV)
            q = q_ref[0, rows, ks]
            k = k_ref[0, rows, ks]
            v = v_ref[0, rows, vs]
            scores = _dot_nt(q, k) * decay_ref[h]
            inner = _dot(scores.astype(BF), v)
            state = state_ref[h]
            cross = _dot(q, state.astype(BF)) * qd_ref[h]
            kk = (k.astype(F32) * kd_ref[h]).astype(BF)
            state_ref[h] = state * cd_ref[h] + _dot_tn(kk, v)
            o = inner + cross
            mu = jnp.mean(o, axis=-1, keepdims=True)
            dlt = o - mu
            var = jnp.mean(dlt * dlt, axis=-1, keepdims=True)
            on = dlt * lax.rsqrt(var + EPS) * gn_ref[:, vs]
            gate = g_ref[0, rows, vs].astype(F32)
            o_ref[0, rows, vs] = (gate * jax.nn.sigmoid(gate) * on).astype(BF)


def _retention(proj, decay, qd, kd, cd, gn, *, batch, seq, chunks_per_step=2):
    c = RET_CHUNK * chunks_per_step
    qk_w = RET_HEADS * RET_DK
    v_w = RET_HEADS * RET_DV
    proj3 = proj.reshape(batch, seq, proj.shape[-1])
    const = lambda a: pl.BlockSpec(a.shape, lambda b, i: (0,) * a.ndim)
    return pl.pallas_call(
        _retention_kernel,
        grid=(batch, seq // c),
        in_specs=[
            pl.BlockSpec((1, c, qk_w), lambda b, i: (b, i, 0)),
            pl.BlockSpec((1, c, qk_w), lambda b, i: (b, i, 1)),
            pl.BlockSpec((1, c, v_w), lambda b, i: (b, i, 1)),
            pl.BlockSpec((1, c, v_w), lambda b, i: (b, i, 2)),
            const(decay), const(qd), const(kd), const(cd), const(gn),
        ],
        out_specs=pl.BlockSpec((1, c, v_w), lambda b, i: (b, i, 0)),
        out_shape=jax.ShapeDtypeStruct((batch, seq, v_w), BF),
        scratch_shapes=[pltpu.VMEM((RET_HEADS, RET_DK, RET_DV), F32)],
        compiler_params=_cparams(2),
        name="retention",
    )(proj3, proj3, proj3, proj3, decay, qd, kd, cd, gn)


FFN_CHUNKS = ((0, 1024), (1024, 2048), (2048, D_FF))


def _mix_ffn_kernel(y_ref, x_ref, wm_ref, gm_ref, g_in_ref, wi_ref, wo_ref, g_out_ref, o_ref):
    x1 = x_ref[...] + _rms_scale(_dot(y_ref[...], wm_ref[...])) * gm_ref[...]
    h = (_rms_scale(x1) * g_in_ref[...]).astype(BF)
    acc = None
    for lo, hi in FFN_CHUNKS:
        gate = _dot(h, wi_ref[:, lo:hi])
        up = _dot(h, wi_ref[:, D_FF + lo:D_FF + hi])
        act = (gate * jax.nn.sigmoid(gate) * up).astype(BF)
        part = _dot(act, wo_ref[lo:hi, :])
        acc = part if acc is None else acc + part
    o_ref[...] = x1 + _rms_scale(acc) * g_out_ref[...]


def _mix_ffn(y2d, x2d, w_mix, g_mix, g_in, w_in, w_out, g_out, *, tm=512):
    n, d = x2d.shape
    kdim = y2d.shape[1]
    resident = lambda a: pl.BlockSpec(a.shape, lambda i: (0,) * a.ndim, pipeline_mode=pl.Buffered(1))
    row = lambda w: pl.BlockSpec((tm, w), lambda i: (i, 0))
    return pl.pallas_call(
        _mix_ffn_kernel,
        grid=(n // tm,),
        in_specs=[row(kdim), row(d), resident(w_mix), resident(g_mix), resident(g_in),
                  resident(w_in), resident(w_out), resident(g_out)],
        out_specs=row(d),
        out_shape=jax.ShapeDtypeStruct((n, d), F32),
        compiler_params=_cparams(1),
        name="mix_ffn",
    )(y2d, x2d, w_mix, g_mix, g_in, w_in, w_out, g_out)


def _kv_proj_kernel(x_ref, g_ref, wn_ref, wt_ref, c_ref, s1_ref, s2_ref,
                    kc_ref, vc_ref, ks_ref, kw_ref, vst_ref, vwt_ref, xs_ref):
    hn = (_rms_scale(x_ref[...]) * g_ref[...]).astype(BF)
    tm = hn.shape[0]
    gw = NSA_GROUPS * NSA_HD
    rn = _dot(hn, wn_ref[...])
    cc, s1, s2 = c_ref[...], s1_ref[...], s2_ref[...]
    lane = lax.broadcasted_iota(jnp.int32, (tm, LANES), 1)
    row = lax.broadcasted_iota(jnp.int32, (tm, LANES), 0)
    blk = (row >> (L_SLC.bit_length() - 1)) & (SEL_BLOCKS - 1)
    indicator = jnp.where(lane == NSA_HD + blk, 1.0, 0.0)
    outs = (kc_ref, vc_ref, ks_ref, kw_ref)
    rotate = (True, False, True, True)
    for sec in range(4):
        for pair in range(gw // LANES):
            col = sec * gw + pair * LANES
            x = rn[:, col:col + LANES]
            if rotate[sec]:
                x = (x * cc + pltpu.roll(x, ROPE_HALF, axis=1) * s1
                     + pltpu.roll(x, LANES - ROPE_HALF, axis=1) * s2)
            if sec < 2:
                n_half = tm // CMP_STRIDE
                xs_ref[...] = x.reshape(n_half, CMP_STRIDE, LANES)
                low = lax.broadcasted_iota(jnp.int32, (n_half, LANES), 1) < NSA_HD
                for pp in range(CMP_STRIDE // 2):
                    ra = xs_ref[:, 2 * pp, :]
                    rb = xs_ref[:, 2 * pp + 1, :]
                    cols = slice(pp * LANES, (pp + 1) * LANES)
                    outs[sec][0, 2 * pair, :, cols] = jnp.where(low, ra, pltpu.roll(rb, NSA_HD, axis=1)).astype(BF)
                    outs[sec][0, 2 * pair + 1, :, cols] = jnp.where(low, pltpu.roll(ra, NSA_HD, axis=1), rb).astype(BF)
            elif sec == 2:
                outs[sec][0, 2 * pair] = jnp.where(lane < NSA_HD, x, indicator).astype(BF)
                outs[sec][0, 2 * pair + 1] = jnp.where(lane < NSA_HD, pltpu.roll(x, NSA_HD, axis=1), indicator).astype(BF)
            else:
                xb = x.astype(BF)
                outs[sec][0, 2 * pair] = xb[:, :NSA_HD]
                outs[sec][0, 2 * pair + 1] = xb[:, NSA_HD:]
    rt = _dot_nt(wt_ref[...], hn).astype(BF)
    ones = jnp.ones((SEL_VROWS - NSA_HD, SEL_KEYS), BF)
    for g in range(NSA_GROUPS):
        for cb in range(tm // SEL_KEYS):
            vst_ref[0, g, cb, 0:NSA_HD, :] = rt[g * NSA_HD:(g + 1) * NSA_HD, cb * SEL_KEYS:(cb + 1) * SEL_KEYS]
            vst_ref[0, g, cb, NSA_HD:SEL_VROWS, :] = ones
        for cb in range(tm // LANES):
            vwt_ref[0, g, cb, 0:NSA_HD, :] = rt[gw + g * NSA_HD:gw + (g + 1) * NSA_HD, cb * LANES:(cb + 1) * LANES]
            vwt_ref[0, g, cb, NSA_HD:SEL_VROWS, :] = ones[:, :LANES]


def _kv_proj(x2d, g, wn, wt, ctab, s1tab, s2tab, *, batch, seq, tm=512):
    n, d = x2d.shape
    tpos = seq // tm
    gdim, hd = NSA_GROUPS, NSA_HD
    normal = jax.ShapeDtypeStruct((batch, gdim, seq, hd), BF)
    normal_spec = pl.BlockSpec((1, gdim, tm, hd), lambda i: (i // tpos, 0, i % tpos, 0))
    tab_spec = pl.BlockSpec((tm, LANES), lambda i: (i % tpos, 0))
    halves = jax.ShapeDtypeStruct((batch, gdim, seq // CMP_STRIDE, CMP_STRIDE * hd), BF)
    half_spec = pl.BlockSpec((1, gdim, tm // CMP_STRIDE, CMP_STRIDE * hd), lambda i: (i // tpos, 0, i % tpos, 0))
    return pl.pallas_call(
        _kv_proj_kernel,
        grid=(n // tm,),
        in_specs=[
            pl.BlockSpec((tm, d), lambda i: (i, 0)),
            pl.BlockSpec((1, d), lambda i: (0, 0)),
            pl.BlockSpec(wn.shape, lambda i: (0, 0)),
            pl.BlockSpec(wt.shape, lambda i: (0, 0)),
            tab_spec, tab_spec, tab_spec,
        ],
        out_specs=[
            half_spec, half_spec,
            pl.BlockSpec((1, gdim, tm, LANES), lambda i: (i // tpos, 0, i % tpos, 0)),
            normal_spec,
            pl.BlockSpec((1, gdim, tm // SEL_KEYS, SEL_VROWS, SEL_KEYS), lambda i: (i // tpos, 0, i % tpos, 0, 0)),
            pl.BlockSpec((1, gdim, tm // LANES, SEL_VROWS, LANES), lambda i: (i // tpos, 0, i % tpos, 0, 0)),
        ],
        out_shape=[
            halves, halves, jax.ShapeDtypeStruct((batch, gdim, seq, LANES), BF), normal,
            jax.ShapeDtypeStruct((batch, gdim, seq // SEL_KEYS, SEL_VROWS, SEL_KEYS), BF),
            jax.ShapeDtypeStruct((batch, gdim, seq // LANES, SEL_VROWS, LANES), BF),
        ],
        scratch_shapes=[pltpu.VMEM((tm // CMP_STRIDE, CMP_STRIDE, LANES), F32)],
        compiler_params=_cparams(1),
        name="kv_proj",
    )(x2d, g, wn, wt, ctab, s1tab, s2tab)


def _gelu_tanh(x):
    return 0.5 * x * (1.0 + jnp.tanh(0.7978845608028654 * (x + 0.044715 * (x * x * x))))


def _compress_kernel(x_ref, pos_ref, w1_ref, w1cat_ref, w2_ref, o_ref, *, transposed_out):
    pb = _dot(pos_ref[...], w1_ref[...])[0:1]
    res = _dot(x_ref[0, 0], w1cat_ref[...])
    rows = res.shape[0]
    nxt = pltpu.roll(res[:, CMP_HIDDEN:], rows - 1, axis=0)
    hid = _gelu_tanh(res[:, :CMP_HIDDEN] + nxt + pb).astype(BF)
    if transposed_out:
        o_ref[0, 0] = _dot_nt(w2_ref[...], hid).astype(BF)
    else:
        o_ref[0, 0] = _dot(hid, w2_ref[...]).astype(BF)


def _compress(x4, pos8, w1, w1cat, w2, *, transposed_out):
    batch, gdim, n_rows, feat = x4.shape
    rows = n_rows
    if transposed_out:
        out_shape = jax.ShapeDtypeStruct((batch, gdim, NSA_HD, n_rows), BF)
        out_spec = pl.BlockSpec((1, 1, NSA_HD, n_rows), lambda b, g: (b, g, 0, 0))
    else:
        out_shape = jax.ShapeDtypeStruct((batch, gdim, n_rows, NSA_HD), BF)
        out_spec = pl.BlockSpec((1, 1, n_rows, NSA_HD), lambda b, g: (b, g, 0, 0))
    kern = functools.partial(_compress_kernel, transposed_out=transposed_out)
    return pl.pallas_call(
        kern,
        grid=(batch, gdim),
        in_specs=[
            pl.BlockSpec((1, 1, rows, feat), lambda b, g: (b, g, 0, 0)),
            pl.BlockSpec(pos8.shape, lambda b, g: (0, 0)),
            pl.BlockSpec(w1.shape, lambda b, g: (0, 0)),
            pl.BlockSpec(w1cat.shape, lambda b, g: (0, 0)),
            pl.BlockSpec(w2.shape, lambda b, g: (0, 0)),
        ],
        out_specs=out_spec,
        out_shape=out_shape,
        compiler_params=_cparams(2),
        name="compress_v" if transposed_out else "compress_k",
    )(x4, pos8, w1, w1cat, w2)


def _nsa_q_proj_kernel(x_ref, g_ref, wq_ref, wg_ref, cos_ref, sin_ref, q_ref, gate_ref):
    hn = (_rms_scale(x_ref[...]) * g_ref[...]).astype(BF)
    tm = hn.shape[0]
    qt = _dot_nt(wq_ref[...], hn) * (NSA_HD ** -0.5 * LOG2E)
    cos, sin = cos_ref[...], sin_ref[...]
    for h in range(NSA_HEADS):
        g, r = divmod(h, NSA_REP)
        base = h * NSA_HD
        x1 = qt[base:base + ROPE_HALF]
        x2 = qt[base + ROPE_HALF:base + ROPE_DIMS]
        head = jnp.concatenate(
            [x1 * cos - x2 * sin, x1 * sin + x2 * cos, qt[base + ROPE_DIMS:base + NSA_HD]], axis=0).astype(BF)
        for blk in range(tm // Q_BLOCK):
            col = blk * NSA_REP * Q_BLOCK + r * Q_BLOCK
            q_ref[0, g, :, col:col + Q_BLOCK] = head[:, blk * Q_BLOCK:(blk + 1) * Q_BLOCK]
    gt = jax.nn.sigmoid(_dot_nt(wg_ref[...], hn))
    for g in range(NSA_GROUPS):
        gate_ref[0, g] = gt[g * 16:(g + 1) * 16]


def _nsa_q_proj(x2d, g, wqt, wgt, cos_t, sin_t, *, batch, seq, tm=512):
    n, d = x2d.shape
    tpos = seq // tm
    qcols = NSA_REP * Q_BLOCK
    return pl.pallas_call(
        _nsa_q_proj_kernel,
        grid=(n // tm,),
        in_specs=[
            pl.BlockSpec((tm, d), lambda i: (i, 0)),
            pl.BlockSpec((1, d), lambda i: (0, 0)),
            pl.BlockSpec(wqt.shape, lambda i: (0, 0)),
            pl.BlockSpec(wgt.shape, lambda i: (0, 0)),
            pl.BlockSpec((ROPE_HALF, tm), lambda i: (0, i % tpos)),
            pl.BlockSpec((ROPE_HALF, tm), lambda i: (0, i % tpos)),
        ],
        out_specs=[
            pl.BlockSpec((1, NSA_GROUPS, NSA_HD, (tm // Q_BLOCK) * qcols), lambda i: (i // tpos, 0, 0, i % tpos)),
            pl.BlockSpec((1, NSA_GROUPS, 16, tm), lambda i: (i // tpos, 0, 0, i % tpos)),
        ],
        out_shape=[
            jax.ShapeDtypeStruct((batch, NSA_GROUPS, NSA_HD, (seq // Q_BLOCK) * qcols), BF),
            jax.ShapeDtypeStruct((batch, NSA_GROUPS, 16, seq), F32),
        ],
        compiler_params=_cparams(1),
        name="nsa_q_proj",
    )(x2d, g, wqt, wgt, cos_t, sin_t)


def _nsa_attn_kernel(q_ref, gate_ref, ck_ref, cl_ref, ks_ref, vst_ref, kw_ref, vwt_ref, o_ref,
                     mt_ref, m_ref, acc_ref, qa_ref, sc_ref):
    gp = q_ref.shape[1]
    bi = pl.program_id(2)
    qcols = NSA_REP * Q_BLOCK
    lane = lax.broadcasted_iota(jnp.int32, (1, qcols), 1)
    t = bi * Q_BLOCK + (lane & (Q_BLOCK - 1))

    n_rows = ck_ref.shape[2]
    n_slc = n_rows // 4
    n_idx = lax.broadcasted_iota(jnp.int32, (n_rows, 1), 0)
    valid = (n_idx * CMP_STRIDE + (L_CMP - 1)) <= t
    w_keys = WINDOW + Q_BLOCK
    wb = jnp.maximum(bi - WINDOW // Q_BLOCK, 0)
    start = pl.multiple_of(wb * Q_BLOCK, Q_BLOCK)
    s_cmp = [_dot(ck_ref[0, g], q_ref[0, g]) for g in range(gp)]
    s_win = [_dot(kw_ref[0, g, pl.ds(start, w_keys), :], q_ref[0, g]) for g in range(gp)]
    oc, imps = [], []
    for g in range(gp):
        s = jnp.where(valid, s_cmp[g], NEG)
        p = jnp.exp2(s - jnp.max(s, axis=0, keepdims=True)).astype(BF)
        res = _dot(cl_ref[0, g], p)
        inv = jnp.where(t >= L_CMP - 1, 1.0 / res[NSA_HD:NSA_HD + 1], 0.0)
        oc.append(res[0:NSA_HD] * inv)
        impu = res[SEL_VROWS:] * inv
        imp_g = impu[:, 0:Q_BLOCK]
        for r in range(1, NSA_REP):
            imp_g = imp_g + impu[:, r * Q_BLOCK:(r + 1) * Q_BLOCK]
        imps.append(imp_g)
    imp = jnp.concatenate(imps, axis=1) if gp > 1 else imps[0]

    j_w = lax.broadcasted_iota(jnp.int32, imp.shape, 0)
    tq = bi * Q_BLOCK + (lax.broadcasted_iota(jnp.int32, (1, imp.shape[1]), 1) & (Q_BLOCK - 1))
    cur = tq >> (L_SLC.bit_length() - 1)
    forced = (j_w == 0) | (j_w == cur) | (j_w == cur - 1)
    val = jnp.where(forced, TAKEN, jnp.where(j_w <= cur, imp, NEG))
    j_f = j_w.astype(F32)
    for _ in range(min(N_SELECT, n_slc) - 3):
        mx = jnp.max(val, axis=0, keepdims=True)
        idx = jnp.min(jnp.where(val == mx, j_f, float(n_slc)), axis=0, keepdims=True)
        val = jnp.where(j_f == idx, TAKEN, val)
    mt = jnp.where(val < 0.5 * TAKEN, 0.0, NEG)
    for g in range(gp):
        for r in range(NSA_REP):
            mt_ref[g, :, r * Q_BLOCK:(r + 1) * Q_BLOCK] = mt[:, g * Q_BLOCK:(g + 1) * Q_BLOCK]

    dist = t - (start + lax.broadcasted_iota(jnp.int32, (w_keys, 1), 0))
    in_window = lax.bitcast_convert_type(dist, jnp.uint32) < jnp.uint32(WINDOW)
    ow = []
    for g in range(gp):
        sw = jnp.where(in_window, s_win[g], NEG)
        pw = jnp.exp2(sw - jnp.max(sw, axis=0, keepdims=True)).astype(BF)
        vw = jnp.concatenate([vwt_ref[0, g, wb + w] for w in range(w_keys // LANES)], axis=1)
        res = _dot(vw, pw)
        ow.append(res[0:NSA_HD] * (1.0 / res[NSA_HD:NSA_HD + 1]))

    for g in range(gp):
        m_ref[g] = jnp.full((1, qcols), NEG, F32)
        acc_ref[g] = jnp.zeros((SEL_VROWS, qcols), F32)
        qa_ref[g, 0:NSA_HD, :] = q_ref[0, g]
        qa_ref[g, NSA_HD:, :] = jnp.zeros((LANES - NSA_HD, qcols), BF)
    row_io = lax.broadcasted_iota(jnp.int32, (SEL_KEYS, 1), 0)
    pad_rows = jnp.zeros((16 - SEL_BLOCKS, qcols), BF)

    def sel_scores(c, slot, g):
        bias = mt_ref[g, pl.ds(pl.multiple_of(c * SEL_BLOCKS, SEL_BLOCKS), SEL_BLOCKS), :].astype(BF)
        qa_ref[g, NSA_HD:NSA_HD + 16, :] = jnp.concatenate([bias, pad_rows], axis=0)
        k = ks_ref[0, g, pl.ds(pl.multiple_of(c * SEL_KEYS, SEL_KEYS), SEL_KEYS), :]
        sc_ref[slot, g] = _dot(k, qa_ref[g])

    def sel_accumulate(c, slot, causal, g):
        sc = sc_ref[slot, g]
        if causal:
            sc = jnp.where((c * SEL_KEYS + row_io) <= t, sc, NEG)
        m_old = m_ref[g]
        m_new = jnp.maximum(m_old, jnp.max(sc, axis=0, keepdims=True))
        alpha = jnp.exp2(m_old - m_new)
        pr = jnp.exp2(sc - m_new).astype(BF)
        acc_ref[g] = alpha * acc_ref[g] + _dot(vst_ref[0, g, c], pr)
        m_ref[g] = m_new

    def sel_pair(k, carry):
        c = 2 * k
        for g in range(gp):
            sel_scores(c + 1, 1, g)
            sel_accumulate(c, 0, False, g)
        for g in range(gp):
            sel_scores(c + 2, 0, g)
            sel_accumulate(c + 1, 1, False, g)
        return carry

    n_full = (bi * Q_BLOCK) >> (SEL_KEYS.bit_length() - 1)
    for g in range(gp):
        sel_scores(0, 0, g)
    lax.fori_loop(0, n_full >> 1, sel_pair, 0)

    @pl.when((n_full & 1) == 0)
    def _():
        for g in range(gp):
            sel_accumulate(n_full, 0, True, g)

    @pl.when((n_full & 1) == 1)
    def _():
        for g in range(gp):
            sel_scores(n_full, 1, g)
            sel_accumulate(n_full - 1, 0, False, g)
        for g in range(gp):
            sel_accumulate(n_full, 1, True, g)

    out_w = NSA_REP * NSA_HD
    for g in range(gp):
        osel = acc_ref[g, 0:NSA_HD, :] * (1.0 / acc_ref[g, NSA_HD:NSA_HD + 1, :])
        heads = []
        for r in range(NSA_REP):
            sl = slice(r * Q_BLOCK, (r + 1) * Q_BLOCK)
            gc = gate_ref[0, g, 0 * NSA_REP + r:0 * NSA_REP + r + 1, :]
            gs = gate_ref[0, g, 1 * NSA_REP + r:1 * NSA_REP + r + 1, :]
            gw = gate_ref[0, g, 2 * NSA_REP + r:2 * NSA_REP + r + 1, :]
            heads.append(gc * oc[g][:, sl] + gs * osel[:, sl] + gw * ow[g][:, sl])
        o_ref[0, :, g * out_w:(g + 1) * out_w] = jnp.concatenate(heads, axis=0).T.astype(BF)


def _nsa_attn(qt, gates, ck, cmp_lhs, ks, vst, kw, vwt, *, batch, seq, gp=4):
    nb = seq // Q_BLOCK
    qcols = NSA_REP * Q_BLOCK
    n_cmp_rows = ck.shape[2]
    hd = NSA_HD
    full = lambda shape: pl.BlockSpec((1, gp) + tuple(shape[2:]), lambda b, g, i: (b, g) + (0,) * (len(shape) - 2),
                                      pipeline_mode=pl.Buffered(1))
    return pl.pallas_call(
        _nsa_attn_kernel,
        grid=(batch, NSA_GROUPS // gp, nb),
        in_specs=[
            pl.BlockSpec((1, gp, hd, qcols), lambda b, g, i: (b, g, 0, i)),
            pl.BlockSpec((1, gp, 16, Q_BLOCK), lambda b, g, i: (b, g, 0, i)),
            full(ck.shape), full(cmp_lhs.shape), full(ks.shape), full(vst.shape), full(kw.shape), full(vwt.shape),
        ],
        out_specs=pl.BlockSpec((1, Q_BLOCK, gp * NSA_REP * hd), lambda b, g, i: (b, i, g)),
        out_shape=jax.ShapeDtypeStruct((batch, seq, NSA_HEADS * hd), BF),
        scratch_shapes=[
            pltpu.VMEM((gp, n_cmp_rows // 4, qcols), F32),
            pltpu.VMEM((gp, 1, qcols), F32),
            pltpu.VMEM((gp, SEL_VROWS, qcols), F32),
            pltpu.VMEM((gp, LANES, qcols), BF),
            pltpu.VMEM((2, gp, SEL_KEYS, qcols), F32),
        ],
        compiler_params=pltpu.CompilerParams(dimension_semantics=("arbitrary",) * 3, vmem_limit_bytes=NSA_VMEM_LIMIT),
        name="nsa_attn",
    )(qt, gates, ck, cmp_lhs, ks, vst, kw, vwt)


def _cos_sin(seq, freqs):
    lo_n = 128
    hi = (jnp.arange(seq // lo_n, dtype=F32) * float(lo_n))[:, None] * freqs[None, :]
    lo = jnp.arange(lo_n, dtype=F32)[:, None] * freqs[None, :]
    ch, sh, cl, sl = jnp.cos(hi)[:, None, :], jnp.sin(hi)[:, None, :], jnp.cos(lo)[None], jnp.sin(lo)[None]
    cos = (ch * cl - sh * sl).reshape(seq, -1)
    sin = (sh * cl + ch * sl).reshape(seq, -1)
    return cos, sin


def _retention_tables(seq):
    c = RET_CHUNK
    freqs = 1.0 / (RET_ROT_BASE ** jnp.linspace(0.0, 1.0, RET_DK // 2, dtype=F32))
    cos, sin = _cos_sin(seq, freqs)
    log_g = jnp.log1p(-jnp.exp2(-5.0 - jnp.arange(RET_HEADS, dtype=F32)))
    idx = jnp.arange(c, dtype=F32)
    diff = idx[:, None] - idx[None, :]
    decay = jnp.where(diff >= 0, jnp.exp(log_g[:, None, None] * jnp.maximum(diff, 0.0)), 0.0)
    qd = jnp.exp(log_g[:, None] * (idx + 1.0))[:, :, None]
    kd = jnp.exp(log_g[:, None] * (c - 1.0 - idx))[:, :, None]
    cd = jnp.exp(log_g * c)[:, None, None]
    return (cos, sin, decay,
            jnp.broadcast_to(qd, (RET_HEADS, c, RET_DV)),
            jnp.broadcast_to(kd, (RET_HEADS, c, RET_DK)),
            jnp.broadcast_to(cd, (RET_HEADS, 1, RET_DV)))


def _overlap_rows(n_slc):
    w = np.zeros((n_slc, 4 * n_slc), np.float32)
    j = np.arange(n_slc)
    for k in range(3):
        w[j, 4 * j + k] = 2.0
    w[j, 4 * j + 3] = 1.0
    w[j[1:], 4 * j[1:] - 1] = 1.0
    return jnp.asarray(w, BF)


def _rope_tables(seq):
    freqs = ROPE_THETA ** (-jnp.arange(0, ROPE_DIMS, 2, dtype=F32) / ROPE_DIMS)
    dim = np.arange(LANES) % NSA_HD
    cos, sin = _cos_sin(seq, freqs[dim % ROPE_HALF])
    ctab = jnp.where(dim < ROPE_DIMS, cos, 1.0)
    s1 = jnp.where((dim >= ROPE_HALF) & (dim < ROPE_DIMS), sin, 0.0)
    s2 = jnp.where(dim < ROPE_HALF, -sin, 0.0)
    return ctab, s1, s2, cos[:, :ROPE_HALF].T, sin[:, :ROPE_HALF].T


def kernel(x, norms, ret_w_in, ret_gn, ret_w_out, nsa_w_in, nsa_w_out, kv_norm, w_kv, cmp_pos_k, cmp_pos_v,
           cmp_w1_k, cmp_w2_k, cmp_w1_v, cmp_w2_v, ffn_w_in, ffn_w_out):
    batch, seq, d = x.shape
    n = batch * seq
    x2d = x.reshape(n, d)
    norm = lambda layer, k: norms[layer, k][None, :]

    cos_r, sin_r, decay, qd, kd, cd = _retention_tables(seq)
    proj = _ret_proj(x2d, norm(0, 0), ret_w_in[0].astype(BF), cos_r, sin_r, seq=seq)
    y = _retention(proj, decay, qd, kd, cd, ret_gn[0][None, :], batch=batch, seq=seq)
    x2d = _mix_ffn(y.reshape(n, -1), x2d, ret_w_out[0].astype(BF), norm(0, 1), norm(0, 2),
                   ffn_w_in[0].astype(BF), ffn_w_out[0].astype(BF), norm(0, 3))

    gw = NSA_GROUPS * NSA_HD
    sec = lambda s: w_kv[:, s * gw:(s + 1) * gw]
    wn = jnp.concatenate([sec(0), sec(1), sec(2), sec(4)], axis=1).astype(BF)
    wt = jnp.concatenate([sec(3), sec(5)], axis=1).T.astype(BF)
    ctab, s1tab, s2tab, cos_t, sin_t = _rope_tables(seq)
    k_cmp, v_cmp, k_slc, k_win, v_slc_t, v_win_t = _kv_proj(
        x2d, kv_norm[None, :], wn, wt, ctab, s1tab, s2tab, batch=batch, seq=seq)

    half = CMP_STRIDE * NSA_HD

    def compress(x4, pos, w1, w2, transposed_out):
        pos8 = jnp.broadcast_to(pos.reshape(1, L_CMP * NSA_HD), (8, L_CMP * NSA_HD)).astype(BF)
        w1cat = jnp.concatenate([w1[:half], w1[half:]], axis=1).astype(BF)
        w2p = (w2.T if transposed_out else w2).astype(BF)
        return _compress(x4, pos8, w1.astype(BF), w1cat, w2p, transposed_out=transposed_out)

    ck = compress(k_cmp, cmp_pos_k, cmp_w1_k, cmp_w2_k, False)
    cvt = compress(v_cmp, cmp_pos_v, cmp_w1_v, cmp_w2_v, True)

    hq = NSA_HEADS * NSA_HD
    wqt = nsa_w_in[0][:, :hq].T.astype(BF)
    wg = nsa_w_in[0][:, hq:].reshape(d, NSA_GROUPS, NSA_REP, 3).transpose(0, 1, 3, 2)
    wg = jnp.pad(wg.reshape(d, NSA_GROUPS, 3 * NSA_REP), ((0, 0), (0, 0), (0, 16 - 3 * NSA_REP)))
    wgt = wg.reshape(d, NSA_GROUPS * 16).T.astype(BF)
    qt, gates = _nsa_q_proj(x2d, norm(1, 0), wqt, wgt, cos_t, sin_t, batch=batch, seq=seq)
    n_cmp_rows = cvt.shape[-1]
    cmp_lhs = jnp.concatenate(
        [cvt, jnp.ones((batch, NSA_GROUPS, SEL_VROWS - NSA_HD, n_cmp_rows), BF),
         jnp.broadcast_to(_overlap_rows(n_cmp_rows // 4), (batch, NSA_GROUPS, n_cmp_rows // 4, n_cmp_rows))],
        axis=2)
    o = _nsa_attn(qt, gates, ck, cmp_lhs, k_slc, v_slc_t, k_win, v_win_t, batch=batch, seq=seq)
    x2d = _mix_ffn(o.reshape(n, -1), x2d, nsa_w_out[0].astype(BF), norm(1, 1), norm(1, 2),
                   ffn_w_in[1].astype(BF), ffn_w_out[1].astype(BF), norm(1, 3))
    return x2d.reshape(batch, seq, d)
```

```python
import functools

import jax
import jax.numpy as jnp
import numpy as np
from jax import lax
from jax.experimental import pallas as pl
from jax.experimental.pallas import tpu as pltpu

BF = jnp.bfloat16
F32 = jnp.float32

D_MODEL = 1024
RET_HEADS = 4
RET_DK = 256
RET_DV = 512
RET_CHUNK = 128
RET_ROT_BASE = 10000.0
NSA_HEADS = 16
NSA_GROUPS = 4
NSA_HD = 64
NSA_REP = 4
L_CMP = 32
CMP_STRIDE = 16
L_SLC = 64
N_SELECT = 16
WINDOW = 512
CMP_HIDDEN = 256
Q_BLOCK = 128
ROPE_THETA = 500000.0
ROPE_DIMS = 16
ROPE_HALF = ROPE_DIMS // 2
D_FF = 2816
EPS = 1e-6
NEG = -1e30
TAKEN = -3e38

LANES = 128
SEL_KEYS = 512
SEL_BLOCKS = SEL_KEYS // L_SLC
BF16_ROWS = 16
SEL_VROWS = NSA_HD + BF16_ROWS
GATE_ROWS = 16
LOG2E = 1.4426950408889634
VMEM_LIMIT = 48 * 1024 * 1024
NSA_VMEM_LIMIT = 58 * 1024 * 1024


def _cparams(n_axes):
    return pltpu.CompilerParams(dimension_semantics=("arbitrary",) * n_axes,
                                vmem_limit_bytes=VMEM_LIMIT)


def _rms_scale(x):
    return x * lax.rsqrt(jnp.mean(x * x, axis=-1, keepdims=True) + EPS)


def _dot(a, b):
    return jnp.dot(a, b, preferred_element_type=F32)


def _dot_nt(a, b):
    return lax.dot_general(a, b, (((1,), (1,)), ((), ())), preferred_element_type=F32)


def _dot_tn(a, b):
    return lax.dot_general(a, b, (((0,), (0,)), ((), ())), preferred_element_type=F32)


def _ret_proj_kernel(x_ref, g_ref, w_ref, cos_ref, sin_ref, o_ref):
    half = RET_DK // 2
    hn = (_rms_scale(x_ref[...]) * g_ref[...]).astype(BF)
    cos = cos_ref[...]
    sin = sin_ref[...]
    for h in range(2 * RET_HEADS):
        acc = _dot(hn, w_ref[:, h * RET_DK:(h + 1) * RET_DK])
        scale = RET_DK ** -0.5 if h >= RET_HEADS else 1.0
        x1 = acc[:, :half]
        x2 = acc[:, half:]
        o_ref[:, h * RET_DK:h * RET_DK + half] = ((x1 * cos - x2 * sin) * scale).astype(BF)
        o_ref[:, h * RET_DK + half:(h + 1) * RET_DK] = ((x1 * sin + x2 * cos) * scale).astype(BF)
    step = 2 * RET_DK
    for c in range(2 * RET_HEADS * RET_DK // step, o_ref.shape[1] // step):
        o_ref[:, c * step:(c + 1) * step] = _dot(hn, w_ref[:, c * step:(c + 1) * step]).astype(BF)


def _ret_proj(x2d, g, w, cos, sin, *, seq, tm=512):
    n, d = x2d.shape
    nout = w.shape[1]
    tpos = seq // tm
    return pl.pallas_call(
        _ret_proj_kernel,
        grid=(n // tm,),
        in_specs=[
            pl.BlockSpec((tm, d), lambda i: (i, 0)),
            pl.BlockSpec((1, d), lambda i: (0, 0)),
            pl.BlockSpec((d, nout), lambda i: (0, 0), pipeline_mode=pl.Buffered(1)),
            pl.BlockSpec((tm, RET_DK // 2), lambda i: (i % tpos, 0)),
            pl.BlockSpec((tm, RET_DK // 2), lambda i: (i % tpos, 0)),
        ],
        out_specs=pl.BlockSpec((tm, nout), lambda i: (i, 0)),
        out_shape=jax.ShapeDtypeStruct((n, nout), BF),
        compiler_params=_cparams(1),
        name="ret_proj",
    )(x2d, g, w, cos, sin)


def _retention_kernel(q_ref, k_ref, v_ref, g_ref, decay_ref, qd_ref, kd_ref, cd_ref, gn_ref, o_ref, state_ref):
    c = pl.program_id(1)

    @pl.when(c == 0)
    def _():
        state_ref[...] = jnp.zeros_like(state_ref)

    for sub in range(q_ref.shape[1] // RET_CHUNK):
        rows = slice(sub * RET_CHUNK, (sub + 1) * RET_CHUNK)
        for h in range(RET_HEADS):
            ks = slice(h * RET_DK, (h + 1) * RET_DK)
            vs = slice(h * RET_DV, (h + 1) * RET_DV)
            q = q_ref[0, rows, ks]
            k = k_ref[0, rows, ks]
            v = v_ref[0, rows, vs]
            scores = _dot_nt(q, k) * decay_ref[h]
            inner = _dot(scores.astype(BF), v)
            state = state_ref[h]
            cross = _dot(q, state.astype(BF)) * qd_ref[h]
            kk = (k.astype(F32) * kd_ref[h]).astype(BF)
            state_ref[h] = state * cd_ref[h] + _dot_tn(kk, v)
            o = inner + cross
            mu = jnp.mean(o, axis=-1, keepdims=True)
            dlt = o - mu
            var = jnp.mean(dlt * dlt, axis=-1, keepdims=True)
            on = dlt * lax.rsqrt(var + EPS) * gn_ref[:, vs]
            gate = g_ref[0, rows, vs].astype(F32)
            o_ref[0, rows, vs] = (gate * jax.nn.sigmoid(gate) * on).astype(BF)


def _retention(proj, decay, qd, kd, cd, gn, *, batch, seq, chunks_per_step=4):
    c = RET_CHUNK * chunks_per_step
    qk_w = RET_HEADS * RET_DK
    v_w = RET_HEADS * RET_DV
    proj3 = proj.reshape(batch, seq, proj.shape[-1])
    const = lambda a: pl.BlockSpec(a.shape, lambda b, i: (0,) * a.ndim)
    return pl.pallas_call(
        _retention_kernel,
        grid=(batch, seq // c),
        in_specs=[
            pl.BlockSpec((1, c, qk_w), lambda b, i: (b, i, 0)),
            pl.BlockSpec((1, c, qk_w), lambda b, i: (b, i, 1)),
            pl.BlockSpec((1, c, v_w), lambda b, i: (b, i, 1)),
            pl.BlockSpec((1, c, v_w), lambda b, i: (b, i, 2)),
            const(decay), const(qd), const(kd), const(cd), const(gn),
        ],
        out_specs=pl.BlockSpec((1, c, v_w), lambda b, i: (b, i, 0)),
        out_shape=jax.ShapeDtypeStruct((batch, seq, v_w), BF),
        scratch_shapes=[pltpu.VMEM((RET_HEADS, RET_DK, RET_DV), F32)],
        compiler_params=_cparams(2),
        name="retention",
    )(proj3, proj3, proj3, proj3, decay, qd, kd, cd, gn)


FFN_CHUNKS = ((0, 1024), (1024, 2048), (2048, D_FF))


def _mix_ffn_kernel(y_ref, x_ref, wm_ref, gm_ref, g_in_ref, wi_ref, wo_ref, g_out_ref, o_ref):
    x1 = x_ref[...] + _rms_scale(_dot(y_ref[...], wm_ref[...])) * gm_ref[...]
    h = (_rms_scale(x1) * g_in_ref[...]).astype(BF)
    acc = None
    for lo, hi in FFN_CHUNKS:
        gate = _dot(h, wi_ref[:, lo:hi])
        up = _dot(h, wi_ref[:, D_FF + lo:D_FF + hi])
        act = (gate * jax.nn.sigmoid(gate) * up).astype(BF)
        part = _dot(act, wo_ref[lo:hi, :])
        acc = part if acc is None else acc + part
    o_ref[...] = x1 + _rms_scale(acc) * g_out_ref[...]


def _mix_ffn(y2d, x2d, w_mix, g_mix, g_in, w_in, w_out, g_out, *, tm=512):
    n, d = x2d.shape
    kdim = y2d.shape[1]
    resident = lambda a: pl.BlockSpec(a.shape, lambda i: (0,) * a.ndim, pipeline_mode=pl.Buffered(1))
    row = lambda w: pl.BlockSpec((tm, w), lambda i: (i, 0))
    return pl.pallas_call(
        _mix_ffn_kernel,
        grid=(n // tm,),
        in_specs=[row(kdim), row(d), resident(w_mix), resident(g_mix), resident(g_in),
                  resident(w_in), resident(w_out), resident(g_out)],
        out_specs=row(d),
        out_shape=jax.ShapeDtypeStruct((n, d), F32),
        compiler_params=_cparams(1),
        name="mix_ffn",
    )(y2d, x2d, w_mix, g_mix, g_in, w_in, w_out, g_out)


def _kv_proj_body(hn, wn_ref, wt_ref, c_ref, s1_ref, s2_ref, kc_ref, vc_ref, ks_ref, kw_ref, vst_ref, vwt_ref, xs_ref):
    tm = hn.shape[0]
    gw = NSA_GROUPS * NSA_HD
    rn = _dot(hn, wn_ref[...])
    cc, s1, s2 = c_ref[...], s1_ref[...], s2_ref[...]
    lane = lax.broadcasted_iota(jnp.int32, (tm, LANES), 1)
    row = lax.broadcasted_iota(jnp.int32, (tm, LANES), 0)
    blk = (row >> (L_SLC.bit_length() - 1)) & (SEL_BLOCKS - 1)
    indicator = jnp.where(lane == NSA_HD + blk, 1.0, 0.0)
    outs = (kc_ref, vc_ref, ks_ref, kw_ref)
    rotate = (True, False, True, True)
    for sec in range(4):
        for pair in range(gw // LANES):
            col = sec * gw + pair * LANES
            x = rn[:, col:col + LANES]
            if rotate[sec]:
                x = (x * cc + pltpu.roll(x, ROPE_HALF, axis=1) * s1
                     + pltpu.roll(x, LANES - ROPE_HALF, axis=1) * s2)
            if sec < 2:
                n_half = tm // CMP_STRIDE
                xs_ref[...] = x.reshape(n_half, CMP_STRIDE, LANES)
                low = lax.broadcasted_iota(jnp.int32, (n_half, LANES), 1) < NSA_HD
                for pp in range(CMP_STRIDE // 2):
                    ra = xs_ref[:, 2 * pp, :]
                    rb = xs_ref[:, 2 * pp + 1, :]
                    cols = slice(pp * LANES, (pp + 1) * LANES)
                    outs[sec][0, 2 * pair, :, cols] = jnp.where(low, ra, pltpu.roll(rb, NSA_HD, axis=1)).astype(BF)
                    outs[sec][0, 2 * pair + 1, :, cols] = jnp.where(low, pltpu.roll(ra, NSA_HD, axis=1), rb).astype(BF)
            elif sec == 2:
                outs[sec][0, 2 * pair] = jnp.where(lane < NSA_HD, x, indicator).astype(BF)
                outs[sec][0, 2 * pair + 1] = jnp.where(lane < NSA_HD, pltpu.roll(x, NSA_HD, axis=1), indicator).astype(BF)
            else:
                xb = x.astype(BF)
                outs[sec][0, 2 * pair] = xb[:, :NSA_HD]
                outs[sec][0, 2 * pair + 1] = xb[:, NSA_HD:]
    rt = _dot_nt(wt_ref[...], hn).astype(BF)
    ones = jnp.ones((SEL_VROWS - NSA_HD, SEL_KEYS), BF)
    for g in range(NSA_GROUPS):
        for cb in range(tm // SEL_KEYS):
            vst_ref[0, g, cb, 0:NSA_HD, :] = rt[g * NSA_HD:(g + 1) * NSA_HD, cb * SEL_KEYS:(cb + 1) * SEL_KEYS]
            vst_ref[0, g, cb, NSA_HD:SEL_VROWS, :] = ones
        for cb in range(tm // LANES):
            vwt_ref[0, g, cb, 0:NSA_HD, :] = rt[gw + g * NSA_HD:gw + (g + 1) * NSA_HD, cb * LANES:(cb + 1) * LANES]
            vwt_ref[0, g, cb, NSA_HD:SEL_VROWS, :] = ones[:, :LANES]


def _gelu_tanh(x):
    return 0.5 * x * (1.0 + jnp.tanh(0.7978845608028654 * (x + 0.044715 * (x * x * x))))


def _compress_kernel(x_ref, pos_ref, w1_ref, w1cat_ref, w2_ref, o_ref, *, transposed_out):
    pb = _dot(pos_ref[...], w1_ref[...])[0:1]
    res = _dot(x_ref[0, 0], w1cat_ref[...])
    rows = res.shape[0]
    nxt = pltpu.roll(res[:, CMP_HIDDEN:], rows - 1, axis=0)
    hid = _gelu_tanh(res[:, :CMP_HIDDEN] + nxt + pb).astype(BF)
    if transposed_out:
        o_ref[0, 0] = _dot_nt(w2_ref[...], hid).astype(BF)
    else:
        o_ref[0, 0] = _dot(hid, w2_ref[...]).astype(BF)


def _compress(x4, pos8, w1, w1cat, w2, *, transposed_out):
    batch, gdim, n_rows, feat = x4.shape
    rows = n_rows
    if transposed_out:
        out_shape = jax.ShapeDtypeStruct((batch, gdim, NSA_HD, n_rows), BF)
        out_spec = pl.BlockSpec((1, 1, NSA_HD, n_rows), lambda b, g: (b, g, 0, 0))
    else:
        out_shape = jax.ShapeDtypeStruct((batch, gdim, n_rows, NSA_HD), BF)
        out_spec = pl.BlockSpec((1, 1, n_rows, NSA_HD), lambda b, g: (b, g, 0, 0))
    kern = functools.partial(_compress_kernel, transposed_out=transposed_out)
    return pl.pallas_call(
        kern,
        grid=(batch, gdim),
        in_specs=[
            pl.BlockSpec((1, 1, rows, feat), lambda b, g: (b, g, 0, 0)),
            pl.BlockSpec(pos8.shape, lambda b, g: (0, 0)),
            pl.BlockSpec(w1.shape, lambda b, g: (0, 0)),
            pl.BlockSpec(w1cat.shape, lambda b, g: (0, 0)),
            pl.BlockSpec(w2.shape, lambda b, g: (0, 0)),
        ],
        out_specs=out_spec,
        out_shape=out_shape,
        compiler_params=_cparams(2),
        name="compress_v" if transposed_out else "compress_k",
    )(x4, pos8, w1, w1cat, w2)


def _q_proj_body(hn, wq_ref, wg_ref, cos_ref, sin_ref, q_ref, gate_ref):
    tm = hn.shape[0]
    qt = _dot_nt(wq_ref[...], hn) * (NSA_HD ** -0.5 * LOG2E)
    cos, sin = cos_ref[...], sin_ref[...]
    for h in range(NSA_HEADS):
        g, r = divmod(h, NSA_REP)
        base = h * NSA_HD
        x1 = qt[base:base + ROPE_HALF]
        x2 = qt[base + ROPE_HALF:base + ROPE_DIMS]
        head = jnp.concatenate(
            [x1 * cos - x2 * sin, x1 * sin + x2 * cos, qt[base + ROPE_DIMS:base + NSA_HD]], axis=0).astype(BF)
        for blk in range(tm // Q_BLOCK):
            col = blk * NSA_REP * Q_BLOCK + r * Q_BLOCK
            q_ref[0, g, :, col:col + Q_BLOCK] = head[:, blk * Q_BLOCK:(blk + 1) * Q_BLOCK]
    gt = jax.nn.sigmoid(_dot_nt(wg_ref[...], hn))
    for g in range(NSA_GROUPS):
        gate_ref[0, g] = gt[g * GATE_ROWS:(g + 1) * GATE_ROWS]


def _nsa_proj_kernel(x_ref, gkv_ref, gq_ref, wn_ref, wt_ref, wq_ref, wg_ref, c_ref, s1_ref, s2_ref, cos_ref, sin_ref,
                     kc_ref, vc_ref, ks_ref, kw_ref, vst_ref, vwt_ref, q_ref, gate_ref, xs_ref):
    xn = _rms_scale(x_ref[...])
    _kv_proj_body((xn * gkv_ref[...]).astype(BF), wn_ref, wt_ref, c_ref, s1_ref, s2_ref,
                  kc_ref, vc_ref, ks_ref, kw_ref, vst_ref, vwt_ref, xs_ref)
    _q_proj_body((xn * gq_ref[...]).astype(BF), wq_ref, wg_ref, cos_ref, sin_ref, q_ref, gate_ref)


def _nsa_proj(x2d, g_kv, g_q, wn, wt, wqt, wgt, ctab, s1tab, s2tab, cos_t, sin_t, *, batch, seq, tm=512):
    n, d = x2d.shape
    tpos = seq // tm
    gdim, hd = NSA_GROUPS, NSA_HD
    qcols = NSA_REP * Q_BLOCK
    bgt = lambda *blk: pl.BlockSpec((1, gdim) + blk, lambda i: (i // tpos, 0, i % tpos) + (0,) * (len(blk) - 1))
    const = lambda a: pl.BlockSpec(a.shape, lambda i: (0,) * a.ndim)
    tab_spec = pl.BlockSpec((tm, LANES), lambda i: (i % tpos, 0))
    tab_t_spec = pl.BlockSpec((ROPE_HALF, tm), lambda i: (0, i % tpos))
    bf = lambda *shape: jax.ShapeDtypeStruct((batch, gdim) + shape, BF)
    return pl.pallas_call(
        _nsa_proj_kernel,
        grid=(n // tm,),
        in_specs=[pl.BlockSpec((tm, d), lambda i: (i, 0)), const(g_kv), const(g_q),
                  const(wn), const(wt), const(wqt), const(wgt),
                  tab_spec, tab_spec, tab_spec, tab_t_spec, tab_t_spec],
        out_specs=[
            bgt(tm // CMP_STRIDE, CMP_STRIDE * hd), bgt(tm // CMP_STRIDE, CMP_STRIDE * hd),
            bgt(tm, LANES), bgt(tm, hd),
            bgt(tm // SEL_KEYS, SEL_VROWS, SEL_KEYS), bgt(tm // LANES, SEL_VROWS, LANES),
            pl.BlockSpec((1, gdim, hd, (tm // Q_BLOCK) * qcols), lambda i: (i // tpos, 0, 0, i % tpos)),
            pl.BlockSpec((1, gdim, GATE_ROWS, tm), lambda i: (i // tpos, 0, 0, i % tpos)),
        ],
        out_shape=[
            bf(seq // CMP_STRIDE, CMP_STRIDE * hd), bf(seq // CMP_STRIDE, CMP_STRIDE * hd),
            bf(seq, LANES), bf(seq, hd),
            bf(seq // SEL_KEYS, SEL_VROWS, SEL_KEYS), bf(seq // LANES, SEL_VROWS, LANES),
            bf(hd, (seq // Q_BLOCK) * qcols),
            jax.ShapeDtypeStruct((batch, gdim, GATE_ROWS, seq), F32),
        ],
        scratch_shapes=[pltpu.VMEM((tm // CMP_STRIDE, CMP_STRIDE, LANES), F32)],
        compiler_params=_cparams(1),
        name="nsa_proj",
    )(x2d, g_kv, g_q, wn, wt, wqt, wgt, ctab, s1tab, s2tab, cos_t, sin_t)


def _nsa_attn_kernel(q_ref, gate_ref, ck_ref, cl_ref, ks_ref, vst_ref, kw_ref, vwt_ref, o_ref,
                     mt_ref, m_ref, acc_ref, qa_ref, sc_ref):
    gp = q_ref.shape[1]
    bi = pl.program_id(2)
    qcols = NSA_REP * Q_BLOCK
    lane = lax.broadcasted_iota(jnp.int32, (1, qcols), 1)
    t = bi * Q_BLOCK + (lane & (Q_BLOCK - 1))

    n_rows = ck_ref.shape[2]
    n_slc = n_rows // 4
    n_idx = lax.broadcasted_iota(jnp.int32, (n_rows, 1), 0)
    valid = (n_idx * CMP_STRIDE + (L_CMP - 1)) <= t
    w_keys = WINDOW + Q_BLOCK
    wb = jnp.maximum(bi - WINDOW // Q_BLOCK, 0)
    start = pl.multiple_of(wb * Q_BLOCK, Q_BLOCK)
    s_cmp = [_dot(ck_ref[0, g], q_ref[0, g]) for g in range(gp)]
    s_win = [_dot(kw_ref[0, g, pl.ds(start, w_keys), :], q_ref[0, g]) for g in range(gp)]
    oc, imps = [], []
    for g in range(gp):
        s = jnp.where(valid, s_cmp[g], NEG)
        p = jnp.exp2(s - jnp.max(s, axis=0, keepdims=True)).astype(BF)
        res = _dot(cl_ref[0, g], p)
        inv = jnp.where(t >= L_CMP - 1, 1.0 / res[NSA_HD:NSA_HD + 1], 0.0)
        oc.append(res[0:NSA_HD] * inv)
        impu = res[SEL_VROWS:] * inv
        imp_g = impu[:, 0:Q_BLOCK]
        for r in range(1, NSA_REP):
            imp_g = imp_g + impu[:, r * Q_BLOCK:(r + 1) * Q_BLOCK]
        imps.append(imp_g)
    imp = jnp.concatenate(imps, axis=1) if gp > 1 else imps[0]

    j_w = lax.broadcasted_iota(jnp.int32, imp.shape, 0)
    tq = bi * Q_BLOCK + (lax.broadcasted_iota(jnp.int32, (1, imp.shape[1]), 1) & (Q_BLOCK - 1))
    cur = tq >> (L_SLC.bit_length() - 1)
    forced = (j_w == 0) | (j_w == cur) | (j_w == cur - 1)
    val = jnp.where(forced, TAKEN, jnp.where(j_w <= cur, imp, NEG))
    j_f = j_w.astype(F32)
    for _ in range(min(N_SELECT, n_slc) - 3):
        mx = jnp.max(val, axis=0, keepdims=True)
        idx = jnp.min(jnp.where(val == mx, j_f, float(n_slc)), axis=0, keepdims=True)
        val = jnp.where(j_f == idx, TAKEN, val)
    mt = jnp.where(val < 0.5 * TAKEN, 0.0, NEG)
    for g in range(gp):
        for r in range(NSA_REP):
            mt_ref[g, :, r * Q_BLOCK:(r + 1) * Q_BLOCK] = mt[:, g * Q_BLOCK:(g + 1) * Q_BLOCK]

    dist = t - (start + lax.broadcasted_iota(jnp.int32, (w_keys, 1), 0))
    in_window = lax.bitcast_convert_type(dist, jnp.uint32) < jnp.uint32(WINDOW)
    ow = []
    for g in range(gp):
        sw = jnp.where(in_window, s_win[g], NEG)
        pw = jnp.exp2(sw - jnp.max(sw, axis=0, keepdims=True)).astype(BF)
        vw = jnp.concatenate([vwt_ref[0, g, wb + w] for w in range(w_keys // LANES)], axis=1)
        res = _dot(vw, pw)
        ow.append(res[0:NSA_HD] * (1.0 / res[NSA_HD:NSA_HD + 1]))

    for g in range(gp):
        m_ref[g] = jnp.full((1, qcols), NEG, F32)
        acc_ref[g] = jnp.zeros((SEL_VROWS, qcols), F32)
        qa_ref[g, 0:NSA_HD, :] = q_ref[0, g]
        qa_ref[g, NSA_HD:, :] = jnp.zeros((LANES - NSA_HD, qcols), BF)
    row_io = lax.broadcasted_iota(jnp.int32, (SEL_KEYS, 1), 0)
    pad_rows = jnp.zeros((BF16_ROWS - SEL_BLOCKS, qcols), BF)

    def sel_scores(c, slot, g):
        bias = mt_ref[g, pl.ds(pl.multiple_of(c * SEL_BLOCKS, SEL_BLOCKS), SEL_BLOCKS), :].astype(BF)
        qa_ref[g, NSA_HD:NSA_HD + BF16_ROWS, :] = jnp.concatenate([bias, pad_rows], axis=0)
        k = ks_ref[0, g, pl.ds(pl.multiple_of(c * SEL_KEYS, SEL_KEYS), SEL_KEYS), :]
        sc_ref[slot, g] = _dot(k, qa_ref[g])

    def sel_accumulate(c, slot, causal, g):
        sc = sc_ref[slot, g]
        if causal:
            sc = jnp.where((c * SEL_KEYS + row_io) <= t, sc, NEG)
        m_old = m_ref[g]
        m_new = jnp.maximum(m_old, jnp.max(sc, axis=0, keepdims=True))
        alpha = jnp.exp2(m_old - m_new)
        pr = jnp.exp2(sc - m_new).astype(BF)
        acc_ref[g] = alpha * acc_ref[g] + _dot(vst_ref[0, g, c], pr)
        m_ref[g] = m_new

    def sel_pair(k, carry):
        c = 2 * k
        for g in range(gp):
            sel_scores(c + 1, 1, g)
            sel_accumulate(c, 0, False, g)
        for g in range(gp):
            sel_scores(c + 2, 0, g)
            sel_accumulate(c + 1, 1, False, g)
        return carry

    n_full = (bi * Q_BLOCK) >> (SEL_KEYS.bit_length() - 1)
    for g in range(gp):
        sel_scores(0, 0, g)
    lax.fori_loop(0, n_full >> 1, sel_pair, 0)

    @pl.when((n_full & 1) == 0)
    def _():
        for g in range(gp):
            sel_accumulate(n_full, 0, True, g)

    @pl.when((n_full & 1) == 1)
    def _():
        for g in range(gp):
            sel_scores(n_full, 1, g)
            sel_accumulate(n_full - 1, 0, False, g)
        for g in range(gp):
            sel_accumulate(n_full, 1, True, g)

    out_w = NSA_REP * NSA_HD
    for g in range(gp):
        osel = acc_ref[g, 0:NSA_HD, :] * (1.0 / acc_ref[g, NSA_HD:NSA_HD + 1, :])
        heads = []
        for r in range(NSA_REP):
            sl = slice(r * Q_BLOCK, (r + 1) * Q_BLOCK)
            gc = gate_ref[0, g, 0 * NSA_REP + r:0 * NSA_REP + r + 1, :]
            gs = gate_ref[0, g, 1 * NSA_REP + r:1 * NSA_REP + r + 1, :]
            gw = gate_ref[0, g, 2 * NSA_REP + r:2 * NSA_REP + r + 1, :]
            heads.append(gc * oc[g][:, sl] + gs * osel[:, sl] + gw * ow[g][:, sl])
        o_ref[0, :, g * out_w:(g + 1) * out_w] = jnp.concatenate(heads, axis=0).T.astype(BF)


def _nsa_attn(qt, gates, ck, cmp_lhs, ks, vst, kw, vwt, *, batch, seq, gp=4):
    nb = seq // Q_BLOCK
    qcols = NSA_REP * Q_BLOCK
    n_cmp_rows = ck.shape[2]
    hd = NSA_HD
    full = lambda shape: pl.BlockSpec((1, gp) + tuple(shape[2:]), lambda b, g, i: (b, g) + (0,) * (len(shape) - 2),
                                      pipeline_mode=pl.Buffered(1))
    return pl.pallas_call(
        _nsa_attn_kernel,
        grid=(batch, NSA_GROUPS // gp, nb),
        in_specs=[
            pl.BlockSpec((1, gp, hd, qcols), lambda b, g, i: (b, g, 0, i)),
            pl.BlockSpec((1, gp, GATE_ROWS, Q_BLOCK), lambda b, g, i: (b, g, 0, i)),
            full(ck.shape), full(cmp_lhs.shape), full(ks.shape), full(vst.shape), full(kw.shape), full(vwt.shape),
        ],
        out_specs=pl.BlockSpec((1, Q_BLOCK, gp * NSA_REP * hd), lambda b, g, i: (b, i, g)),
        out_shape=jax.ShapeDtypeStruct((batch, seq, NSA_HEADS * hd), BF),
        scratch_shapes=[
            pltpu.VMEM((gp, n_cmp_rows // 4, qcols), F32),
            pltpu.VMEM((gp, 1, qcols), F32),
            pltpu.VMEM((gp, SEL_VROWS, qcols), F32),
            pltpu.VMEM((gp, LANES, qcols), BF),
            pltpu.VMEM((2, gp, SEL_KEYS, qcols), F32),
        ],
        compiler_params=pltpu.CompilerParams(dimension_semantics=("arbitrary",) * 3, vmem_limit_bytes=NSA_VMEM_LIMIT),
        name="nsa_attn",
    )(qt, gates, ck, cmp_lhs, ks, vst, kw, vwt)


def _cos_sin(seq, freqs):
    lo_n = 128
    hi = (jnp.arange(seq // lo_n, dtype=F32) * float(lo_n))[:, None] * freqs[None, :]
    lo = jnp.arange(lo_n, dtype=F32)[:, None] * freqs[None, :]
    ch, sh, cl, sl = jnp.cos(hi)[:, None, :], jnp.sin(hi)[:, None, :], jnp.cos(lo)[None], jnp.sin(lo)[None]
    cos = (ch * cl - sh * sl).reshape(seq, -1)
    sin = (sh * cl + ch * sl).reshape(seq, -1)
    return cos, sin


def _retention_tables(seq):
    c = RET_CHUNK
    freqs = 1.0 / (RET_ROT_BASE ** jnp.linspace(0.0, 1.0, RET_DK // 2, dtype=F32))
    cos, sin = _cos_sin(seq, freqs)
    log_g = jnp.log1p(-jnp.exp2(-5.0 - jnp.arange(RET_HEADS, dtype=F32)))
    idx = jnp.arange(c, dtype=F32)
    diff = idx[:, None] - idx[None, :]
    decay = jnp.where(diff >= 0, jnp.exp(log_g[:, None, None] * jnp.maximum(diff, 0.0)), 0.0)
    qd = jnp.exp(log_g[:, None] * (idx + 1.0))[:, :, None]
    kd = jnp.exp(log_g[:, None] * (c - 1.0 - idx))[:, :, None]
    cd = jnp.exp(log_g * c)[:, None, None]
    return (cos, sin, decay,
            jnp.broadcast_to(qd, (RET_HEADS, c, RET_DV)),
            jnp.broadcast_to(kd, (RET_HEADS, c, RET_DK)),
            jnp.broadcast_to(cd, (RET_HEADS, 1, RET_DV)))


def _overlap_rows(n_slc):
    w = np.zeros((n_slc, 4 * n_slc), np.float32)
    j = np.arange(n_slc)
    for k in range(3):
        w[j, 4 * j + k] = 2.0
    w[j, 4 * j + 3] = 1.0
    w[j[1:], 4 * j[1:] - 1] = 1.0
    return jnp.asarray(w, BF)


def _rope_tables(seq):
    freqs = ROPE_THETA ** (-jnp.arange(0, ROPE_DIMS, 2, dtype=F32) / ROPE_DIMS)
    dim = np.arange(LANES) % NSA_HD
    cos, sin = _cos_sin(seq, freqs[dim % ROPE_HALF])
    ctab = jnp.where(dim < ROPE_DIMS, cos, 1.0)
    s1 = jnp.where((dim >= ROPE_HALF) & (dim < ROPE_DIMS), sin, 0.0)
    s2 = jnp.where(dim < ROPE_HALF, -sin, 0.0)
    return ctab, s1, s2, cos[:, :ROPE_HALF].T, sin[:, :ROPE_HALF].T


def kernel(x, norms, ret_w_in, ret_gn, ret_w_out, nsa_w_in, nsa_w_out, kv_norm, w_kv, cmp_pos_k, cmp_pos_v,
           cmp_w1_k, cmp_w2_k, cmp_w1_v, cmp_w2_v, ffn_w_in, ffn_w_out):
    batch, seq, d = x.shape
    n = batch * seq
    x2d = x.reshape(n, d)
    norm = lambda layer, k: norms[layer, k][None, :]

    cos_r, sin_r, decay, qd, kd, cd = _retention_tables(seq)
    proj = _ret_proj(x2d, norm(0, 0), ret_w_in[0].astype(BF), cos_r, sin_r, seq=seq)
    y = _retention(proj, decay, qd, kd, cd, ret_gn[0][None, :], batch=batch, seq=seq)
    x2d = _mix_ffn(y.reshape(n, -1), x2d, ret_w_out[0].astype(BF), norm(0, 1), norm(0, 2),
                   ffn_w_in[0].astype(BF), ffn_w_out[0].astype(BF), norm(0, 3))

    gw = NSA_GROUPS * NSA_HD
    sec = lambda s: w_kv[:, s * gw:(s + 1) * gw]
    wn = jnp.concatenate([sec(0), sec(1), sec(2), sec(4)], axis=1).astype(BF)
    wt = jnp.concatenate([sec(3), sec(5)], axis=1).T.astype(BF)
    ctab, s1tab, s2tab, cos_t, sin_t = _rope_tables(seq)
    hq = NSA_HEADS * NSA_HD
    wqt = nsa_w_in[0][:, :hq].T.astype(BF)
    wg = nsa_w_in[0][:, hq:].reshape(d, NSA_GROUPS, NSA_REP, 3).transpose(0, 1, 3, 2)
    wg = jnp.pad(wg.reshape(d, NSA_GROUPS, 3 * NSA_REP), ((0, 0), (0, 0), (0, GATE_ROWS - 3 * NSA_REP)))
    wgt = wg.reshape(d, NSA_GROUPS * GATE_ROWS).T.astype(BF)
    k_cmp, v_cmp, k_slc, k_win, v_slc_t, v_win_t, qt, gates = _nsa_proj(
        x2d, kv_norm[None, :], norm(1, 0), wn, wt, wqt, wgt, ctab, s1tab, s2tab, cos_t, sin_t, batch=batch, seq=seq)

    half = CMP_STRIDE * NSA_HD

    def compress(x4, pos, w1, w2, transposed_out):
        pos8 = jnp.broadcast_to(pos.reshape(1, L_CMP * NSA_HD), (8, L_CMP * NSA_HD)).astype(BF)
        w1cat = jnp.concatenate([w1[:half], w1[half:]], axis=1).astype(BF)
        w2p = (w2.T if transposed_out else w2).astype(BF)
        return _compress(x4, pos8, w1.astype(BF), w1cat, w2p, transposed_out=transposed_out)

    ck = compress(k_cmp, cmp_pos_k, cmp_w1_k, cmp_w2_k, False)
    cvt = compress(v_cmp, cmp_pos_v, cmp_w1_v, cmp_w2_v, True)

    n_cmp_rows = cvt.shape[-1]
    cmp_lhs = jnp.concatenate(
        [cvt, jnp.ones((batch, NSA_GROUPS, SEL_VROWS - NSA_HD, n_cmp_rows), BF),
         jnp.broadcast_to(_overlap_rows(n_cmp_rows // 4), (batch, NSA_GROUPS, n_cmp_rows // 4, n_cmp_rows))],
        axis=2)
    o = _nsa_attn(qt, gates, ck, cmp_lhs, k_slc, v_slc_t, k_win, v_win_t, batch=batch, seq=seq)
    x2d = _mix_ffn(o.reshape(n, -1), x2d, nsa_w_out[0].astype(BF), norm(1, 1), norm(1, 2),
                   ffn_w_in[1].astype(BF), ffn_w_out[1].astype(BF), norm(1, 3))
    return x2d.reshape(batch, seq, d)
```

```python
import functools

import jax
import jax.numpy as jnp
import numpy as np
from jax import lax
from jax.experimental import pallas as pl
from jax.experimental.pallas import tpu as pltpu

BF = jnp.bfloat16
F32 = jnp.float32

D_MODEL = 1024
RET_HEADS = 4
RET_DK = 256
RET_DV = 512
RET_CHUNK = 128
RET_ROT_BASE = 10000.0
NSA_HEADS = 16
NSA_GROUPS = 4
NSA_HD = 64
NSA_REP = 4
L_CMP = 32
CMP_STRIDE = 16
L_SLC = 64
N_SELECT = 16
WINDOW = 512
CMP_HIDDEN = 256
Q_BLOCK = 128
ROPE_THETA = 500000.0
ROPE_DIMS = 16
ROPE_HALF = ROPE_DIMS // 2
D_FF = 2816
EPS = 1e-6
NEG = -1e30
TAKEN = -3e38

LANES = 128
SEL_KEYS = 512
SEL_BLOCKS = SEL_KEYS // L_SLC
BF16_ROWS = 16
SEL_VROWS = NSA_HD + BF16_ROWS
GATE_ROWS = 16
LOG2E = 1.4426950408889634
VMEM_LIMIT = 48 * 1024 * 1024
NSA_VMEM_LIMIT = 58 * 1024 * 1024


def _cparams(n_axes):
    return pltpu.CompilerParams(dimension_semantics=("arbitrary",) * n_axes,
                                vmem_limit_bytes=VMEM_LIMIT)


def _rms_scale(x):
    return x * lax.rsqrt(jnp.mean(x * x, axis=-1, keepdims=True) + EPS)


def _dot(a, b):
    return jnp.dot(a, b, preferred_element_type=F32)


def _dot_nt(a, b):
    return lax.dot_general(a, b, (((1,), (1,)), ((), ())), preferred_element_type=F32)


def _dot_tn(a, b):
    return lax.dot_general(a, b, (((0,), (0,)), ((), ())), preferred_element_type=F32)


def _ret_proj_kernel(x_ref, g_ref, w_ref, cos_ref, sin_ref, o_ref):
    half = RET_DK // 2
    hn = (_rms_scale(x_ref[...]) * g_ref[...]).astype(BF)
    cos = cos_ref[...]
    sin = sin_ref[...]
    for h in range(2 * RET_HEADS):
        acc = _dot(hn, w_ref[:, h * RET_DK:(h + 1) * RET_DK])
        scale = RET_DK ** -0.5 if h >= RET_HEADS else 1.0
        x1 = acc[:, :half]
        x2 = acc[:, half:]
        o_ref[:, h * RET_DK:h * RET_DK + half] = ((x1 * cos - x2 * sin) * scale).astype(BF)
        o_ref[:, h * RET_DK + half:(h + 1) * RET_DK] = ((x1 * sin + x2 * cos) * scale).astype(BF)
    step = 2 * RET_DK
    for c in range(2 * RET_HEADS * RET_DK // step, o_ref.shape[1] // step):
        o_ref[:, c * step:(c + 1) * step] = _dot(hn, w_ref[:, c * step:(c + 1) * step]).astype(BF)


def _ret_proj(x2d, g, w, cos, sin, *, seq, tm=512):
    n, d = x2d.shape
    nout = w.shape[1]
    tpos = seq // tm
    return pl.pallas_call(
        _ret_proj_kernel,
        grid=(n // tm,),
        in_specs=[
            pl.BlockSpec((tm, d), lambda i: (i, 0)),
            pl.BlockSpec((1, d), lambda i: (0, 0)),
            pl.BlockSpec((d, nout), lambda i: (0, 0), pipeline_mode=pl.Buffered(1)),
            pl.BlockSpec((tm, RET_DK // 2), lambda i: (i % tpos, 0)),
            pl.BlockSpec((tm, RET_DK // 2), lambda i: (i % tpos, 0)),
        ],
        out_specs=pl.BlockSpec((tm, nout), lambda i: (i, 0)),
        out_shape=jax.ShapeDtypeStruct((n, nout), BF),
        compiler_params=_cparams(1),
        name="ret_proj",
    )(x2d, g, w, cos, sin)


def _retention_kernel(q_ref, k_ref, v_ref, g_ref, decay_ref, qd_ref, kd_ref, cd_ref, gn_ref, o_ref, state_ref):
    c = pl.program_id(1)

    @pl.when(c == 0)
    def _():
        state_ref[...] = jnp.zeros_like(state_ref)

    for sub in range(q_ref.shape[1] // RET_CHUNK):
        rows = slice(sub * RET_CHUNK, (sub + 1) * RET_CHUNK)
        for h in range(RET_HEADS):
            ks = slice(h * RET_DK, (h + 1) * RET_DK)
            vs = slice(h * RET_DV, (h + 1) * RET_DV)
            q = q_ref[0, rows, ks]
            k = k_ref[0, rows, ks]
            v = v_ref[0, rows, vs]
            scores = _dot_nt(q, k) * decay_ref[h]
            inner = _dot(scores.astype(BF), v)
            state = state_ref[h]
            cross = _dot(q, state.astype(BF)) * qd_ref[h]
            kk = (k.astype(F32) * kd_ref[h]).astype(BF)
            state_ref[h] = state * cd_ref[h] + _dot_tn(kk, v)
            o = inner + cross
            mu = jnp.mean(o, axis=-1, keepdims=True)
            dlt = o - mu
            var = jnp.mean(dlt * dlt, axis=-1, keepdims=True)
            on = dlt * lax.rsqrt(var + EPS) * gn_ref[:, vs]
            gate = g_ref[0, rows, vs]
            o_ref[0, rows, vs] = (gate * jax.nn.sigmoid(gate)) * on.astype(BF)


def _retention(proj, decay, qd, kd, cd, gn, *, batch, seq, chunks_per_step=4):
    c = RET_CHUNK * chunks_per_step
    qk_w = RET_HEADS * RET_DK
    v_w = RET_HEADS * RET_DV
    proj3 = proj.reshape(batch, seq, proj.shape[-1])
    const = lambda a: pl.BlockSpec(a.shape, lambda b, i: (0,) * a.ndim)
    return pl.pallas_call(
        _retention_kernel,
        grid=(batch, seq // c),
        in_specs=[
            pl.BlockSpec((1, c, qk_w), lambda b, i: (b, i, 0)),
            pl.BlockSpec((1, c, qk_w), lambda b, i: (b, i, 1)),
            pl.BlockSpec((1, c, v_w), lambda b, i: (b, i, 1)),
            pl.BlockSpec((1, c, v_w), lambda b, i: (b, i, 2)),
            const(decay), const(qd), const(kd), const(cd), const(gn),
        ],
        out_specs=pl.BlockSpec((1, c, v_w), lambda b, i: (b, i, 0)),
        out_shape=jax.ShapeDtypeStruct((batch, seq, v_w), BF),
        scratch_shapes=[pltpu.VMEM((RET_HEADS, RET_DK, RET_DV), F32)],
        compiler_params=_cparams(2),
        name="retention",
    )(proj3, proj3, proj3, proj3, decay, qd, kd, cd, gn)


FFN_CHUNKS = ((0, 1024), (1024, 2048), (2048, D_FF))


def _mix_ffn_kernel(y_ref, x_ref, wm_ref, gm_ref, g_in_ref, wi_ref, wo_ref, g_out_ref, o_ref):
    x1 = x_ref[...] + _rms_scale(_dot(y_ref[...], wm_ref[...])) * gm_ref[...]
    h = (_rms_scale(x1) * g_in_ref[...]).astype(BF)
    acc = None
    for lo, hi in FFN_CHUNKS:
        gate = _dot(h, wi_ref[:, lo:hi])
        up = _dot(h, wi_ref[:, D_FF + lo:D_FF + hi])
        act = (gate * jax.nn.sigmoid(gate) * up).astype(BF)
        part = _dot(act, wo_ref[lo:hi, :])
        acc = part if acc is None else acc + part
    o_ref[...] = x1 + _rms_scale(acc) * g_out_ref[...]


def _mix_ffn(y2d, x2d, w_mix, g_mix, g_in, w_in, w_out, g_out, *, tm=512):
    n, d = x2d.shape
    kdim = y2d.shape[1]
    resident = lambda a: pl.BlockSpec(a.shape, lambda i: (0,) * a.ndim, pipeline_mode=pl.Buffered(1))
    row = lambda w: pl.BlockSpec((tm, w), lambda i: (i, 0))
    return pl.pallas_call(
        _mix_ffn_kernel,
        grid=(n // tm,),
        in_specs=[row(kdim), row(d), resident(w_mix), resident(g_mix), resident(g_in),
                  resident(w_in), resident(w_out), resident(g_out)],
        out_specs=row(d),
        out_shape=jax.ShapeDtypeStruct((n, d), F32),
        compiler_params=_cparams(1),
        name="mix_ffn",
    )(y2d, x2d, w_mix, g_mix, g_in, w_in, w_out, g_out)


def _kv_proj_body(hn, wn_ref, wt_ref, c_ref, s1_ref, s2_ref, kc_ref, vc_ref, ks_ref, kw_ref, vst_ref, vwt_ref, xs_ref):
    tm = hn.shape[0]
    gw = NSA_GROUPS * NSA_HD
    rn = _dot(hn, wn_ref[...])
    cc, s1, s2 = c_ref[...], s1_ref[...], s2_ref[...]
    lane = lax.broadcasted_iota(jnp.int32, (tm, LANES), 1)
    row = lax.broadcasted_iota(jnp.int32, (tm, LANES), 0)
    blk = (row >> (L_SLC.bit_length() - 1)) & (SEL_BLOCKS - 1)
    indicator = jnp.where(lane == NSA_HD + blk, 1.0, 0.0)
    outs = (kc_ref, vc_ref, ks_ref, kw_ref)
    rotate = (True, False, True, True)
    for sec in range(4):
        for pair in range(gw // LANES):
            col = sec * gw + pair * LANES
            x = rn[:, col:col + LANES]
            if rotate[sec]:
                x = (x * cc + pltpu.roll(x, ROPE_HALF, axis=1) * s1
                     + pltpu.roll(x, LANES - ROPE_HALF, axis=1) * s2)
            if sec < 2:
                n_half = tm // CMP_STRIDE
                xs_ref[...] = x.reshape(n_half, CMP_STRIDE, LANES)
                low = lax.broadcasted_iota(jnp.int32, (n_half, LANES), 1) < NSA_HD
                for pp in range(CMP_STRIDE // 2):
                    ra = xs_ref[:, 2 * pp, :]
                    rb = xs_ref[:, 2 * pp + 1, :]
                    cols = slice(pp * LANES, (pp + 1) * LANES)
                    outs[sec][0, 2 * pair, :, cols] = jnp.where(low, ra, pltpu.roll(rb, NSA_HD, axis=1)).astype(BF)
                    outs[sec][0, 2 * pair + 1, :, cols] = jnp.where(low, pltpu.roll(ra, NSA_HD, axis=1), rb).astype(BF)
            elif sec == 2:
                outs[sec][0, 2 * pair] = jnp.where(lane < NSA_HD, x, indicator).astype(BF)
                outs[sec][0, 2 * pair + 1] = jnp.where(lane < NSA_HD, pltpu.roll(x, NSA_HD, axis=1), indicator).astype(BF)
            else:
                xb = x.astype(BF)
                outs[sec][0, 2 * pair] = xb[:, :NSA_HD]
                outs[sec][0, 2 * pair + 1] = xb[:, NSA_HD:]
    rt = _dot_nt(wt_ref[...], hn).astype(BF)
    ones = jnp.ones((SEL_VROWS - NSA_HD, SEL_KEYS), BF)
    for g in range(NSA_GROUPS):
        for cb in range(tm // SEL_KEYS):
            vst_ref[0, g, cb, 0:NSA_HD, :] = rt[g * NSA_HD:(g + 1) * NSA_HD, cb * SEL_KEYS:(cb + 1) * SEL_KEYS]
            vst_ref[0, g, cb, NSA_HD:SEL_VROWS, :] = ones
        for cb in range(tm // LANES):
            vwt_ref[0, g, cb, 0:NSA_HD, :] = rt[gw + g * NSA_HD:gw + (g + 1) * NSA_HD, cb * LANES:(cb + 1) * LANES]
            vwt_ref[0, g, cb, NSA_HD:SEL_VROWS, :] = ones[:, :LANES]


def _gelu_tanh(x):
    return 0.5 * x * (1.0 + jnp.tanh(0.7978845608028654 * (x + 0.044715 * (x * x * x))))


def _compress_kernel(x_ref, pos_ref, w1_ref, w1cat_ref, w2_ref, o_ref, *, transposed_out):
    pb = _dot(pos_ref[...], w1_ref[...])[0:1]
    res = _dot(x_ref[0, 0], w1cat_ref[...])
    rows = res.shape[0]
    nxt = pltpu.roll(res[:, CMP_HIDDEN:], rows - 1, axis=0)
    hid = _gelu_tanh(res[:, :CMP_HIDDEN] + nxt + pb).astype(BF)
    if transposed_out:
        o_ref[0, 0] = _dot_nt(w2_ref[...], hid).astype(BF)
    else:
        o_ref[0, 0] = _dot(hid, w2_ref[...]).astype(BF)


def _compress(x4, pos8, w1, w1cat, w2, *, transposed_out):
    batch, gdim, n_rows, feat = x4.shape
    rows = n_rows
    if transposed_out:
        out_shape = jax.ShapeDtypeStruct((batch, gdim, NSA_HD, n_rows), BF)
        out_spec = pl.BlockSpec((1, 1, NSA_HD, n_rows), lambda b, g: (b, g, 0, 0))
    else:
        out_shape = jax.ShapeDtypeStruct((batch, gdim, n_rows, NSA_HD), BF)
        out_spec = pl.BlockSpec((1, 1, n_rows, NSA_HD), lambda b, g: (b, g, 0, 0))
    kern = functools.partial(_compress_kernel, transposed_out=transposed_out)
    return pl.pallas_call(
        kern,
        grid=(batch, gdim),
        in_specs=[
            pl.BlockSpec((1, 1, rows, feat), lambda b, g: (b, g, 0, 0)),
            pl.BlockSpec(pos8.shape, lambda b, g: (0, 0)),
            pl.BlockSpec(w1.shape, lambda b, g: (0, 0)),
            pl.BlockSpec(w1cat.shape, lambda b, g: (0, 0)),
            pl.BlockSpec(w2.shape, lambda b, g: (0, 0)),
        ],
        out_specs=out_spec,
        out_shape=out_shape,
        compiler_params=_cparams(2),
        name="compress_v" if transposed_out else "compress_k",
    )(x4, pos8, w1, w1cat, w2)


def _q_proj_body(hn, wq_ref, wg_ref, cos_ref, sin_ref, q_ref, gate_ref):
    tm = hn.shape[0]
    qt = _dot_nt(wq_ref[...], hn) * (NSA_HD ** -0.5 * LOG2E)
    cos, sin = cos_ref[...], sin_ref[...]
    for h in range(NSA_HEADS):
        g, r = divmod(h, NSA_REP)
        base = h * NSA_HD
        x1 = qt[base:base + ROPE_HALF]
        x2 = qt[base + ROPE_HALF:base + ROPE_DIMS]
        head = jnp.concatenate(
            [x1 * cos - x2 * sin, x1 * sin + x2 * cos, qt[base + ROPE_DIMS:base + NSA_HD]], axis=0).astype(BF)
        for blk in range(tm // Q_BLOCK):
            col = blk * NSA_REP * Q_BLOCK + r * Q_BLOCK
            q_ref[0, g, :, col:col + Q_BLOCK] = head[:, blk * Q_BLOCK:(blk + 1) * Q_BLOCK]
    gt = jax.nn.sigmoid(_dot_nt(wg_ref[...], hn))
    for g in range(NSA_GROUPS):
        gate_ref[0, g] = gt[g * GATE_ROWS:(g + 1) * GATE_ROWS]


def _nsa_proj_kernel(x_ref, gkv_ref, gq_ref, wn_ref, wt_ref, wq_ref, wg_ref, c_ref, s1_ref, s2_ref, cos_ref, sin_ref,
                     kc_ref, vc_ref, ks_ref, kw_ref, vst_ref, vwt_ref, q_ref, gate_ref, xs_ref):
    xn = _rms_scale(x_ref[...])
    _kv_proj_body((xn * gkv_ref[...]).astype(BF), wn_ref, wt_ref, c_ref, s1_ref, s2_ref,
                  kc_ref, vc_ref, ks_ref, kw_ref, vst_ref, vwt_ref, xs_ref)
    _q_proj_body((xn * gq_ref[...]).astype(BF), wq_ref, wg_ref, cos_ref, sin_ref, q_ref, gate_ref)


def _nsa_proj(x2d, g_kv, g_q, wn, wt, wqt, wgt, ctab, s1tab, s2tab, cos_t, sin_t, *, batch, seq, tm=512):
    n, d = x2d.shape
    tpos = seq // tm
    gdim, hd = NSA_GROUPS, NSA_HD
    qcols = NSA_REP * Q_BLOCK
    bgt = lambda *blk: pl.BlockSpec((1, gdim) + blk, lambda i: (i // tpos, 0, i % tpos) + (0,) * (len(blk) - 1))
    const = lambda a: pl.BlockSpec(a.shape, lambda i: (0,) * a.ndim)
    tab_spec = pl.BlockSpec((tm, LANES), lambda i: (i % tpos, 0))
    tab_t_spec = pl.BlockSpec((ROPE_HALF, tm), lambda i: (0, i % tpos))
    bf = lambda *shape: jax.ShapeDtypeStruct((batch, gdim) + shape, BF)
    return pl.pallas_call(
        _nsa_proj_kernel,
        grid=(n // tm,),
        in_specs=[pl.BlockSpec((tm, d), lambda i: (i, 0)), const(g_kv), const(g_q),
                  const(wn), const(wt), const(wqt), const(wgt),
                  tab_spec, tab_spec, tab_spec, tab_t_spec, tab_t_spec],
        out_specs=[
            bgt(tm // CMP_STRIDE, CMP_STRIDE * hd), bgt(tm // CMP_STRIDE, CMP_STRIDE * hd),
            bgt(tm, LANES), bgt(tm, hd),
            bgt(tm // SEL_KEYS, SEL_VROWS, SEL_KEYS), bgt(tm // LANES, SEL_VROWS, LANES),
            pl.BlockSpec((1, gdim, hd, (tm // Q_BLOCK) * qcols), lambda i: (i // tpos, 0, 0, i % tpos)),
            pl.BlockSpec((1, gdim, GATE_ROWS, tm), lambda i: (i // tpos, 0, 0, i % tpos)),
        ],
        out_shape=[
            bf(seq // CMP_STRIDE, CMP_STRIDE * hd), bf(seq // CMP_STRIDE, CMP_STRIDE * hd),
            bf(seq, LANES), bf(seq, hd),
            bf(seq // SEL_KEYS, SEL_VROWS, SEL_KEYS), bf(seq // LANES, SEL_VROWS, LANES),
            bf(hd, (seq // Q_BLOCK) * qcols),
            jax.ShapeDtypeStruct((batch, gdim, GATE_ROWS, seq), F32),
        ],
        scratch_shapes=[pltpu.VMEM((tm // CMP_STRIDE, CMP_STRIDE, LANES), F32)],
        compiler_params=_cparams(1),
        name="nsa_proj",
    )(x2d, g_kv, g_q, wn, wt, wqt, wgt, ctab, s1tab, s2tab, cos_t, sin_t)


def _nsa_attn_kernel(q_ref, gate_ref, ck_ref, cl_ref, ks_ref, vst_ref, kw_ref, vwt_ref, o_ref,
                     mt_ref, m_ref, acc_ref, qa_ref, sc_ref):
    gp = q_ref.shape[1]
    bi = pl.program_id(2)
    qcols = NSA_REP * Q_BLOCK
    lane = lax.broadcasted_iota(jnp.int32, (1, qcols), 1)
    t = bi * Q_BLOCK + (lane & (Q_BLOCK - 1))

    n_rows = ck_ref.shape[2]
    n_slc = n_rows // 4
    n_idx = lax.broadcasted_iota(jnp.int32, (n_rows, 1), 0)
    valid = (n_idx * CMP_STRIDE + (L_CMP - 1)) <= t
    w_keys = WINDOW + Q_BLOCK
    wb = jnp.maximum(bi - WINDOW // Q_BLOCK, 0)
    start = pl.multiple_of(wb * Q_BLOCK, Q_BLOCK)
    s_cmp = [_dot(ck_ref[0, g], q_ref[0, g]) for g in range(gp)]
    s_win = [_dot(kw_ref[0, g, pl.ds(start, w_keys), :], q_ref[0, g]) for g in range(gp)]
    oc, imps = [], []
    for g in range(gp):
        s = jnp.where(valid, s_cmp[g], NEG)
        p = jnp.exp2(s - jnp.max(s, axis=0, keepdims=True)).astype(BF)
        res = _dot(cl_ref[0, g], p)
        inv = jnp.where(t >= L_CMP - 1, 1.0 / res[NSA_HD:NSA_HD + 1], 0.0)
        oc.append(res[0:NSA_HD] * inv)
        impu = res[SEL_VROWS:] * inv
        imp_g = impu[:, 0:Q_BLOCK]
        for r in range(1, NSA_REP):
            imp_g = imp_g + impu[:, r * Q_BLOCK:(r + 1) * Q_BLOCK]
        imps.append(imp_g)
    imp = jnp.concatenate(imps, axis=1) if gp > 1 else imps[0]

    j_w = lax.broadcasted_iota(jnp.int32, imp.shape, 0)
    tq = bi * Q_BLOCK + (lax.broadcasted_iota(jnp.int32, (1, imp.shape[1]), 1) & (Q_BLOCK - 1))
    cur = tq >> (L_SLC.bit_length() - 1)
    forced = (j_w == 0) | (j_w == cur) | (j_w == cur - 1)
    val = jnp.where(forced, TAKEN, jnp.where(j_w <= cur, imp, NEG))
    j_f = j_w.astype(F32)
    for _ in range(min(N_SELECT, n_slc) - 3):
        mx = jnp.max(val, axis=0, keepdims=True)
        idx = jnp.min(jnp.where(val == mx, j_f, float(n_slc)), axis=0, keepdims=True)
        val = jnp.where(j_f == idx, TAKEN, val)
    mt = jnp.where(val < 0.5 * TAKEN, 0.0, NEG)
    for g in range(gp):
        for r in range(NSA_REP):
            mt_ref[g, :, r * Q_BLOCK:(r + 1) * Q_BLOCK] = mt[:, g * Q_BLOCK:(g + 1) * Q_BLOCK]

    dist = t - (start + lax.broadcasted_iota(jnp.int32, (w_keys, 1), 0))
    in_window = lax.bitcast_convert_type(dist, jnp.uint32) < jnp.uint32(WINDOW)
    ow = []
    for g in range(gp):
        sw = jnp.where(in_window, s_win[g], NEG)
        pw = jnp.exp2(sw - jnp.max(sw, axis=0, keepdims=True)).astype(BF)
        vw = jnp.concatenate([vwt_ref[0, g, wb + w] for w in range(w_keys // LANES)], axis=1)
        res = _dot(vw, pw)
        ow.append(res[0:NSA_HD] * (1.0 / res[NSA_HD:NSA_HD + 1]))

    for g in range(gp):
        m_ref[g] = jnp.full((1, qcols), NEG, F32)
        acc_ref[g] = jnp.zeros((SEL_VROWS, qcols), F32)
        qa_ref[g, 0:NSA_HD, :] = q_ref[0, g]
        qa_ref[g, NSA_HD:, :] = jnp.zeros((LANES - NSA_HD, qcols), BF)
    row_io = lax.broadcasted_iota(jnp.int32, (SEL_KEYS, 1), 0)
    pad_rows = jnp.zeros((BF16_ROWS - SEL_BLOCKS, qcols), BF)

    def sel_scores(c, slot, g):
        bias = mt_ref[g, pl.ds(pl.multiple_of(c * SEL_BLOCKS, SEL_BLOCKS), SEL_BLOCKS), :].astype(BF)
        qa_ref[g, NSA_HD:NSA_HD + BF16_ROWS, :] = jnp.concatenate([bias, pad_rows], axis=0)
        k = ks_ref[0, g, pl.ds(pl.multiple_of(c * SEL_KEYS, SEL_KEYS), SEL_KEYS), :]
        sc_ref[slot, g] = _dot(k, qa_ref[g])

    half_w = qcols // 2

    def sel_accumulate(c, slot, causal, g):
        for hf in range(2):
            cols = slice(hf * half_w, (hf + 1) * half_w)
            sc = sc_ref[slot, g, :, cols]
            if causal:
                sc = jnp.where((c * SEL_KEYS + row_io) <= t[:, cols], sc, NEG)
            m_old = m_ref[g, :, cols]
            m_new = jnp.maximum(m_old, jnp.max(sc, axis=0, keepdims=True))
            alpha = jnp.exp2(m_old - m_new)
            pr = jnp.exp2(sc - m_new).astype(BF)
            acc_ref[g, :, cols] = alpha * acc_ref[g, :, cols] + _dot(vst_ref[0, g, c], pr)
            m_ref[g, :, cols] = m_new

    def sel_pair(k, carry):
        c = 2 * k
        for g in range(gp):
            sel_scores(c + 1, 1, g)
            sel_accumulate(c, 0, False, g)
        for g in range(gp):
            sel_scores(c + 2, 0, g)
            sel_accumulate(c + 1, 1, False, g)
        return carry

    n_full = (bi * Q_BLOCK) >> (SEL_KEYS.bit_length() - 1)
    for g in range(gp):
        sel_scores(0, 0, g)
    lax.fori_loop(0, n_full >> 1, sel_pair, 0)

    @pl.when((n_full & 1) == 0)
    def _():
        for g in range(gp):
            sel_accumulate(n_full, 0, True, g)

    @pl.when((n_full & 1) == 1)
    def _():
        for g in range(gp):
            sel_scores(n_full, 1, g)
            sel_accumulate(n_full - 1, 0, False, g)
        for g in range(gp):
            sel_accumulate(n_full, 1, True, g)

    out_w = NSA_REP * NSA_HD
    for g in range(gp):
        osel = acc_ref[g, 0:NSA_HD, :] * (1.0 / acc_ref[g, NSA_HD:NSA_HD + 1, :])
        heads = []
        for r in range(NSA_REP):
            sl = slice(r * Q_BLOCK, (r + 1) * Q_BLOCK)
            gc = gate_ref[0, g, 0 * NSA_REP + r:0 * NSA_REP + r + 1, :]
            gs = gate_ref[0, g, 1 * NSA_REP + r:1 * NSA_REP + r + 1, :]
            gw = gate_ref[0, g, 2 * NSA_REP + r:2 * NSA_REP + r + 1, :]
            heads.append(gc * oc[g][:, sl] + gs * osel[:, sl] + gw * ow[g][:, sl])
        o_ref[0, :, g * out_w:(g + 1) * out_w] = jnp.concatenate(heads, axis=0).T.astype(BF)


def _nsa_attn(qt, gates, ck, cmp_lhs, ks, vst, kw, vwt, *, batch, seq, gp=4):
    nb = seq // Q_BLOCK
    qcols = NSA_REP * Q_BLOCK
    n_cmp_rows = ck.shape[2]
    hd = NSA_HD
    full = lambda shape: pl.BlockSpec((1, gp) + tuple(shape[2:]), lambda b, g, i: (b, g) + (0,) * (len(shape) - 2),
                                      pipeline_mode=pl.Buffered(1))
    return pl.pallas_call(
        _nsa_attn_kernel,
        grid=(batch, NSA_GROUPS // gp, nb),
        in_specs=[
            pl.BlockSpec((1, gp, hd, qcols), lambda b, g, i: (b, g, 0, i)),
            pl.BlockSpec((1, gp, GATE_ROWS, Q_BLOCK), lambda b, g, i: (b, g, 0, i)),
            full(ck.shape), full(cmp_lhs.shape), full(ks.shape), full(vst.shape), full(kw.shape), full(vwt.shape),
        ],
        out_specs=pl.BlockSpec((1, Q_BLOCK, gp * NSA_REP * hd), lambda b, g, i: (b, i, g)),
        out_shape=jax.ShapeDtypeStruct((batch, seq, NSA_HEADS * hd), BF),
        scratch_shapes=[
            pltpu.VMEM((gp, n_cmp_rows // 4, qcols), F32),
            pltpu.VMEM((gp, 1, qcols), F32),
            pltpu.VMEM((gp, SEL_VROWS, qcols), F32),
            pltpu.VMEM((gp, LANES, qcols), BF),
            pltpu.VMEM((2, gp, SEL_KEYS, qcols), F32),
        ],
        compiler_params=pltpu.CompilerParams(dimension_semantics=("arbitrary",) * 3, vmem_limit_bytes=NSA_VMEM_LIMIT),
        name="nsa_attn",
    )(qt, gates, ck, cmp_lhs, ks, vst, kw, vwt)


def _cos_sin(seq, freqs):
    lo_n = 128
    hi = (jnp.arange(seq // lo_n, dtype=F32) * float(lo_n))[:, None] * freqs[None, :]
    lo = jnp.arange(lo_n, dtype=F32)[:, None] * freqs[None, :]
    ch, sh, cl, sl = jnp.cos(hi)[:, None, :], jnp.sin(hi)[:, None, :], jnp.cos(lo)[None], jnp.sin(lo)[None]
    cos = (ch * cl - sh * sl).reshape(seq, -1)
    sin = (sh * cl + ch * sl).reshape(seq, -1)
    return cos, sin


def _retention_tables(seq):
    c = RET_CHUNK
    freqs = 1.0 / (RET_ROT_BASE ** jnp.linspace(0.0, 1.0, RET_DK // 2, dtype=F32))
    cos, sin = _cos_sin(seq, freqs)
    log_g = jnp.log1p(-jnp.exp2(-5.0 - jnp.arange(RET_HEADS, dtype=F32)))
    idx = jnp.arange(c, dtype=F32)
    diff = idx[:, None] - idx[None, :]
    decay = jnp.where(diff >= 0, jnp.exp(log_g[:, None, None] * jnp.maximum(diff, 0.0)), 0.0)
    qd = jnp.exp(log_g[:, None] * (idx + 1.0))[:, :, None]
    kd = jnp.exp(log_g[:, None] * (c - 1.0 - idx))[:, :, None]
    cd = jnp.exp(log_g * c)[:, None, None]
    return (cos, sin, decay,
            jnp.broadcast_to(qd, (RET_HEADS, c, RET_DV)),
            jnp.broadcast_to(kd, (RET_HEADS, c, RET_DK)),
            jnp.broadcast_to(cd, (RET_HEADS, 1, RET_DV)))


def _overlap_rows(n_slc):
    w = np.zeros((n_slc, 4 * n_slc), np.float32)
    j = np.arange(n_slc)
    for k in range(3):
        w[j, 4 * j + k] = 2.0
    w[j, 4 * j + 3] = 1.0
    w[j[1:], 4 * j[1:] - 1] = 1.0
    return jnp.asarray(w, BF)


def _rope_tables(seq):
    freqs = ROPE_THETA ** (-jnp.arange(0, ROPE_DIMS, 2, dtype=F32) / ROPE_DIMS)
    dim = np.arange(LANES) % NSA_HD
    cos, sin = _cos_sin(seq, freqs[dim % ROPE_HALF])
    ctab = jnp.where(dim < ROPE_DIMS, cos, 1.0)
    s1 = jnp.where((dim >= ROPE_HALF) & (dim < ROPE_DIMS), sin, 0.0)
    s2 = jnp.where(dim < ROPE_HALF, -sin, 0.0)
    return ctab, s1, s2, cos[:, :ROPE_HALF].T, sin[:, :ROPE_HALF].T


def kernel(x, norms, ret_w_in, ret_gn, ret_w_out, nsa_w_in, nsa_w_out, kv_norm, w_kv, cmp_pos_k, cmp_pos_v,
           cmp_w1_k, cmp_w2_k, cmp_w1_v, cmp_w2_v, ffn_w_in, ffn_w_out):
    batch, seq, d = x.shape
    n = batch * seq
    x2d = x.reshape(n, d)
    norm = lambda layer, k: norms[layer, k][None, :]

    cos_r, sin_r, decay, qd, kd, cd = _retention_tables(seq)
    proj = _ret_proj(x2d, norm(0, 0), ret_w_in[0].astype(BF), cos_r, sin_r, seq=seq)
    y = _retention(proj, decay, qd, kd, cd, ret_gn[0][None, :], batch=batch, seq=seq)
    x2d = _mix_ffn(y.reshape(n, -1), x2d, ret_w_out[0].astype(BF), norm(0, 1), norm(0, 2),
                   ffn_w_in[0].astype(BF), ffn_w_out[0].astype(BF), norm(0, 3))

    gw = NSA_GROUPS * NSA_HD
    sec = lambda s: w_kv[:, s * gw:(s + 1) * gw]
    wn = jnp.concatenate([sec(0), sec(1), sec(2), sec(4)], axis=1).astype(BF)
    wt = jnp.concatenate([sec(3), sec(5)], axis=1).T.astype(BF)
    ctab, s1tab, s2tab, cos_t, sin_t = _rope_tables(seq)
    hq = NSA_HEADS * NSA_HD
    wqt = nsa_w_in[0][:, :hq].T.astype(BF)
    wg = nsa_w_in[0][:, hq:].reshape(d, NSA_GROUPS, NSA_REP, 3).transpose(0, 1, 3, 2)
    wg = jnp.pad(wg.reshape(d, NSA_GROUPS, 3 * NSA_REP), ((0, 0), (0, 0), (0, GATE_ROWS - 3 * NSA_REP)))
    wgt = wg.reshape(d, NSA_GROUPS * GATE_ROWS).T.astype(BF)
    k_cmp, v_cmp, k_slc, k_win, v_slc_t, v_win_t, qt, gates = _nsa_proj(
        x2d, kv_norm[None, :], norm(1, 0), wn, wt, wqt, wgt, ctab, s1tab, s2tab, cos_t, sin_t, batch=batch, seq=seq)

    half = CMP_STRIDE * NSA_HD

    def compress(x4, pos, w1, w2, transposed_out):
        pos8 = jnp.broadcast_to(pos.reshape(1, L_CMP * NSA_HD), (8, L_CMP * NSA_HD)).astype(BF)
        w1cat = jnp.concatenate([w1[:half], w1[half:]], axis=1).astype(BF)
        w2p = (w2.T if transposed_out else w2).astype(BF)
        return _compress(x4, pos8, w1.astype(BF), w1cat, w2p, transposed_out=transposed_out)

    ck = compress(k_cmp, cmp_pos_k, cmp_w1_k, cmp_w2_k, False)
    cvt = compress(v_cmp, cmp_pos_v, cmp_w1_v, cmp_w2_v, True)

    n_cmp_rows = cvt.shape[-1]
    cmp_lhs = jnp.concatenate(
        [cvt, jnp.ones((batch, NSA_GROUPS, SEL_VROWS - NSA_HD, n_cmp_rows), BF),
         jnp.broadcast_to(_overlap_rows(n_cmp_rows // 4), (batch, NSA_GROUPS, n_cmp_rows // 4, n_cmp_rows))],
        axis=2)
    o = _nsa_attn(qt, gates, ck, cmp_lhs, k_slc, v_slc_t, k_win, v_win_t, batch=batch, seq=seq)
    x2d = _mix_ffn(o.reshape(n, -1), x2d, nsa_w_out[0].astype(BF), norm(1, 1), norm(1, 2),
                   ffn_w_in[1].astype(BF), ffn_w_out[1].astype(BF), norm(1, 3))
    return x2d.reshape(batch, seq, d)
```
